```python
import jax, jax.numpy as jnp
from jax import lax
import numpy as np

D_MODEL = 1024
BATCH = 8
SEQ = 4096
DEPTH = 4

N_MIXERS = 2
EXPAND = 2
D_BRANCH = EXPAND * D_MODEL
CHUNK = 128
GMLP_GROUPS = 8
GMLP_GROUP_W = D_BRANCH // GMLP_GROUPS
LRU_HEADS = 8
LRU_HEAD_W = D_BRANCH // LRU_HEADS
CONV_WIDTH = 4
LRU_C = 8.0
N_A = (DEPTH + 1) // 2
N_B = DEPTH // 2
EPS = 1e-6

kernel_name = "hybrid_gmlp_rglru_sandwich_adaln"


def rms_norm(x, g):
    xf = x.astype(jnp.float32)
    y = xf * lax.rsqrt(jnp.mean(xf * xf, axis=-1, keepdims=True) + EPS)
    return (y * g.astype(jnp.float32)).astype(x.dtype)


def layer_norm(x, g):
    xf = x.astype(jnp.float32)
    xc = xf - jnp.mean(xf, axis=-1, keepdims=True)
    y = xc * lax.rsqrt(jnp.mean(xc * xc, axis=-1, keepdims=True) + EPS)
    return (y * g.astype(jnp.float32)).astype(x.dtype)


def gmlp_mixer(h, w_in, v_norm, w_s, b_s, w_out):
    b, s, _ = h.shape
    z = h @ w_in
    u, v, g = jnp.split(z, 3, axis=-1)
    u = jax.nn.gelu(u)
    v = layer_norm(jax.nn.gelu(v), v_norm)
    n_chunks = s // CHUNK
    v = v.reshape(b, n_chunks, CHUNK, GMLP_GROUPS, GMLP_GROUP_W)
    causal = jnp.tril(jnp.ones((CHUNK, CHUNK), dtype=bool))
    w = jnp.where(causal[None], w_s, jnp.zeros_like(w_s))
    mixed = jnp.einsum('gts,bnsgc->bntgc', w, v) + jnp.transpose(b_s)[None, None, :, :, None]
    y = u * mixed.reshape(b, s, D_BRANCH) * jax.nn.silu(g)
    return y @ w_out


def _lru_combine(left, right):
    a1, b1 = left
    a2, b2 = right
    return a1 * a2, a2 * b1 + b2


def rglru_mixer(h, w_in, conv_w, conv_b, ga_w, ga_b, gx_w, gx_b, lam, w_out):
    b, s, _ = h.shape
    z = h @ w_in
    xb, g = jnp.split(z, 2, axis=-1)
    xp = jnp.pad(xb, ((0, 0), (CONV_WIDTH - 1, 0), (0, 0)))
    xc = conv_b + xp[:, 0:s] * conv_w[0]
    for k in range(1, CONV_WIDTH):
        xc = xc + xp[:, k:k + s] * conv_w[k]
    xh = xc.reshape(b, s, LRU_HEADS, LRU_HEAD_W)
    r = jax.nn.sigmoid(jnp.einsum('bshi,hij->bshj', xh, ga_w).reshape(b, s, D_BRANCH) + ga_b)
    i = jax.nn.sigmoid(jnp.einsum('bshi,hij->bshj', xh, gx_w).reshape(b, s, D_BRANCH) + gx_b)
    log_a = -LRU_C * r.astype(jnp.float32) * jax.nn.softplus(-lam.astype(jnp.float32))
    a = jnp.exp(log_a)
    mult = jnp.sqrt(-jnp.expm1(2.0 * log_a))
    bterm = mult * (i * xc).astype(jnp.float32)
    _, hs = lax.associative_scan(_lru_combine, (a, bterm), axis=1)
    y = hs.astype(h.dtype) * jax.nn.silu(g)
    return y @ w_out


def setup_inputs(seed: int = 0) -> dict:
    key = jax.random.key(seed)
    ks = jax.random.split(key, 24)
    f32 = jnp.float32
    nrm = lambda k, shape, scale: jax.random.normal(k, shape, f32) * scale
    u_lam = jax.random.uniform(ks[23], (N_B, D_BRANCH), f32, minval=0.9, maxval=0.999)
    a_base = u_lam ** (1.0 / LRU_C)
    lam = jnp.log(a_base) - jnp.log1p(-a_base)
    return {
        "x": nrm(ks[0], (BATCH, SEQ, D_MODEL), 1.0),
        "c": nrm(ks[1], (BATCH, D_MODEL), 1.0),
        "mod_w": nrm(ks[2], (DEPTH, D_MODEL, 3 * D_MODEL), D_MODEL ** -0.5),
        "mod_b": nrm(ks[3], (DEPTH, 3 * D_MODEL), 0.02),
        "pre_norm": 1.0 + nrm(ks[4], (DEPTH, D_MODEL), 0.1),
        "post_norm": 1.0 + nrm(ks[5], (DEPTH, D_MODEL), 0.1),
        "a_w_in": nrm(ks[6], (N_A, D_MODEL, 3 * D_BRANCH), D_MODEL ** -0.5),
        "a_v_norm": 1.0 + nrm(ks[7], (N_A, D_BRANCH), 0.1),
        "a_w_s": nrm(ks[8], (N_A, GMLP_GROUPS, CHUNK, CHUNK), CHUNK ** -0.5),
        "a_b_s": 1.0 + nrm(ks[9], (N_A, GMLP_GROUPS, CHUNK), 0.1),
        "a_w_out": nrm(ks[10], (N_A, D_BRANCH, D_MODEL), D_BRANCH ** -0.5),
        "b_w_in": nrm(ks[11], (N_B, D_MODEL, 2 * D_BRANCH), D_MODEL ** -0.5),
        "b_conv_w": nrm(ks[12], (N_B, CONV_WIDTH, D_BRANCH), CONV_WIDTH ** -0.5),
        "b_conv_b": nrm(ks[13], (N_B, D_BRANCH), 0.01),
        "b_ga_w": nrm(ks[14], (N_B, LRU_HEADS, LRU_HEAD_W, LRU_HEAD_W), LRU_HEAD_W ** -0.5),
        "b_ga_b": nrm(ks[15], (N_B, D_BRANCH), 0.01),
        "b_gx_w": nrm(ks[16], (N_B, LRU_HEADS, LRU_HEAD_W, LRU_HEAD_W), LRU_HEAD_W ** -0.5),
        "b_gx_b": nrm(ks[17], (N_B, D_BRANCH), 0.01),
        "b_lambda": lam,
        "b_w_out": nrm(ks[18], (N_B, D_BRANCH, D_MODEL), D_BRANCH ** -0.5),
    }


def reference(x, c, mod_w, mod_b, pre_norm, post_norm,
              a_w_in, a_v_norm, a_w_s, a_b_s, a_w_out,
              b_w_in, b_conv_w, b_conv_b, b_ga_w, b_ga_b, b_gx_w, b_gx_b, b_lambda, b_w_out):
    cond = jax.nn.silu(c)
    for layer in range(DEPTH):
        mod = cond @ mod_w[layer] + mod_b[layer]
        shift, scale, gate = jnp.split(mod[:, None, :], 3, axis=-1)
        h = rms_norm(x, pre_norm[layer]) * (1.0 + scale) + shift
        j = layer // N_MIXERS
        if layer % N_MIXERS == 0:
            y = gmlp_mixer(h, a_w_in[j], a_v_norm[j], a_w_s[j], a_b_s[j], a_w_out[j])
        else:
            y = rglru_mixer(h, b_w_in[j], b_conv_w[j], b_conv_b[j], b_ga_w[j], b_ga_b[j],
                            b_gx_w[j], b_gx_b[j], b_lambda[j], b_w_out[j])
        x = x + gate * rms_norm(y, post_norm[layer])
    return x
```

```python
import functools
import math

import jax
import jax.numpy as jnp
from jax.experimental import pallas as pl
from jax.experimental.pallas import tpu as pltpu

EPS = 1e-6
CHUNK = 128
GROUPS = 8
CONV_WIDTH = 4
LRU_C = 8.0
VMEM_LIMIT_BYTES = 56 * 1024 * 1024

_GELU_C0 = math.sqrt(2.0 / math.pi)
_GELU_C1 = 0.044715
_LOG2E = 1.0 / math.log(2.0)


def _gelu(x):
    inner = x * (x * x * (_GELU_C0 * _GELU_C1) + _GELU_C0)
    hx = 0.5 * x
    return hx * jnp.tanh(inner) + hx


def _silu(x):
    hx = 0.5 * x
    return hx * jnp.tanh(hx) + hx


def _sigmoid(x):
    return 0.5 * jnp.tanh(0.5 * x) + 0.5


def _dot(a, b):
    return jnp.dot(a, b, preferred_element_type=jnp.float32)


def _mod_kernel(c_ref, w_ref, b_ref, o_ref):
    c = c_ref[...]
    cond = c * _sigmoid(c)
    o_ref[...] = _dot(cond.astype(jnp.bfloat16), w_ref[...].astype(jnp.bfloat16)) + b_ref[...]


def _mod_call(c, mod_w, mod_b):
    depth, d, d3 = mod_w.shape
    b = c.shape[0]
    nb = d3 // d
    return pl.pallas_call(
        _mod_kernel,
        grid=(depth, nb),
        in_specs=[
            pl.BlockSpec((b, d), lambda l, n: (0, 0)),
            pl.BlockSpec((None, d, d), lambda l, n: (l, 0, n)),
            pl.BlockSpec((None, 1, d), lambda l, n: (l, 0, n)),
        ],
        out_specs=pl.BlockSpec((None, b, d), lambda l, n: (l, 0, n)),
        out_shape=jax.ShapeDtypeStruct((depth, b, d3), jnp.float32),
        compiler_params=pltpu.CompilerParams(
            dimension_semantics=("arbitrary", "arbitrary"),
            vmem_limit_bytes=VMEM_LIMIT_BYTES),
        name="adaln_mod",
    )(c, mod_w, mod_b.reshape(depth, 1, d3))


def _layer_a_kernel(x_ref, mod_ref, pre_ref, post_ref, win_ref, vnorm_ref, ws_ref, bs_ref,
                    wout_ref, o_ref, h_scr, v_scr, y_scr, wm_scr, *, tm, d, e):
    gw = e // GROUPS

    @pl.when((pl.program_id(0) == 0) & (pl.program_id(1) == 0))
    def _():
        row = jax.lax.broadcasted_iota(jnp.int32, (CHUNK, CHUNK), 0)
        col = jax.lax.broadcasted_iota(jnp.int32, (CHUNK, CHUNK), 1)
        for g in range(GROUPS):
            wm_scr[g] = jnp.where(col <= row, ws_ref[g], 0.0).astype(jnp.bfloat16)

    x = x_ref[...]
    mod = mod_ref[...]
    shift, scale, gate = mod[:, :d], mod[:, d:2 * d], mod[:, 2 * d:]
    ms = jnp.mean(x * x, axis=-1, keepdims=True)
    h = x * jax.lax.rsqrt(ms + EPS) * (pre_ref[...] * (1.0 + scale)) + shift
    h_scr[...] = h.astype(jnp.bfloat16)

    s1 = jnp.zeros((tm, 1), jnp.float32)
    s2 = jnp.zeros((tm, 1), jnp.float32)
    for g in range(GROUPS):
        vc = _gelu(_dot(h_scr[...], win_ref[:, e + g * gw:e + (g + 1) * gw]))
        v_scr[:, g * gw:(g + 1) * gw] = vc
        s1 = s1 + jnp.sum(vc, axis=-1, keepdims=True)
        s2 = s2 + jnp.sum(vc * vc, axis=-1, keepdims=True)
    mean = s1 * (1.0 / e)
    var = s2 * (1.0 / e) - mean * mean
    rstd = jax.lax.rsqrt(var + EPS)

    for g in range(GROUPS):
        cols = slice(g * gw, (g + 1) * gw)
        vn = ((v_scr[:, cols] - mean) * rstd * vnorm_ref[:, cols]).astype(jnp.bfloat16)
        bias = bs_ref[g]
        bias = jnp.concatenate([bias] * (gw // 128), axis=1)
        parts = []
        for j in range(tm // CHUNK):
            mx = _dot(wm_scr[g], vn[j * CHUNK:(j + 1) * CHUNK, :]) + bias
            parts.append(mx)
        mixed = jnp.concatenate(parts, axis=0) if len(parts) > 1 else parts[0]
        u = _gelu(_dot(h_scr[...], win_ref[:, cols]))
        gt = _silu(_dot(h_scr[...], win_ref[:, 2 * e + g * gw:2 * e + (g + 1) * gw]))
        y_scr[:, cols] = (u * mixed * gt).astype(jnp.bfloat16)

    y = _dot(y_scr[...], wout_ref[...])
    ms2 = jnp.mean(y * y, axis=-1, keepdims=True)
    o_ref[...] = x + y * jax.lax.rsqrt(ms2 + EPS) * (post_ref[...] * gate)


def _layer_a(x, mod_l, pre, post, w_in, v_norm, w_s, b_s, w_out, *, tm):
    b, s, d = x.shape
    e = w_out.shape[0]
    const2 = lambda bi, i: (0, 0)
    const3 = lambda bi, i: (0, 0, 0)
    single = pl.Buffered(1)
    kern = functools.partial(_layer_a_kernel, tm=tm, d=d, e=e)
    return pl.pallas_call(
        kern,
        grid=(b, s // tm),
        in_specs=[
            pl.BlockSpec((None, tm, d), lambda bi, i: (bi, i, 0)),
            pl.BlockSpec((None, 1, 3 * d), lambda bi, i: (bi, 0, 0)),
            pl.BlockSpec((1, d), const2, pipeline_mode=single),
            pl.BlockSpec((1, d), const2, pipeline_mode=single),
            pl.BlockSpec((d, 3 * e), const2, pipeline_mode=single),
            pl.BlockSpec((1, e), const2, pipeline_mode=single),
            pl.BlockSpec((GROUPS, CHUNK, CHUNK), const3, pipeline_mode=single),
            pl.BlockSpec((GROUPS, CHUNK, 128), const3, pipeline_mode=single),
            pl.BlockSpec((e, d), const2, pipeline_mode=single),
        ],
        out_specs=pl.BlockSpec((None, tm, d), lambda bi, i: (bi, i, 0)),
        out_shape=jax.ShapeDtypeStruct((b, s, d), jnp.float32),
        scratch_shapes=[
            pltpu.VMEM((tm, d), jnp.bfloat16),
            pltpu.VMEM((tm, e), jnp.float32),
            pltpu.VMEM((tm, e), jnp.bfloat16),
            pltpu.VMEM((GROUPS, CHUNK, CHUNK), jnp.bfloat16),
        ],
        compiler_params=pltpu.CompilerParams(
            dimension_semantics=("arbitrary", "arbitrary"),
            vmem_limit_bytes=VMEM_LIMIT_BYTES),
        name="gmlp_layer",
    )(x, mod_l.reshape(b, 1, 3 * d), pre.reshape(1, d), post.reshape(1, d),
      w_in.astype(jnp.bfloat16), v_norm.reshape(1, e), w_s,
      jnp.broadcast_to(b_s[:, :, None], (GROUPS, CHUNK, 128)),
      w_out.astype(jnp.bfloat16))


def _layer_b_kernel(x_ref, mod_ref, pre_ref, post_ref, win_ref, cw_ref, cb_ref, gaw_ref,
                    gab_ref, gxw_ref, gxb_ref, lam_ref, wout_ref, o_ref,
                    h_scr, xb_scr, hs_scr, y_scr, state_scr, *, tt, nb, d, e):
    hw = e // GROUPS
    tm = tt * nb
    pad = CONV_WIDTH - 1

    @pl.when(pl.program_id(0) == 0)
    def _():
        xb_scr[0:pad] = jnp.zeros((pad, nb, e), jnp.float32)
        state_scr[...] = jnp.zeros((nb, e), jnp.float32)

    x = x_ref[...]
    mod = mod_ref[...]
    shift, scale, gate = mod[:, :d], mod[:, d:2 * d], mod[:, 2 * d:]
    ms = jnp.mean(x * x, axis=-1, keepdims=True)
    h = x * jax.lax.rsqrt(ms + EPS) * (pre_ref[...] * (1.0 + scale))[None] + shift[None]
    h_scr[...] = h.reshape(tm, d).astype(jnp.bfloat16)

    lam = lam_ref[...]
    kl = (-LRU_C * _LOG2E) * (jnp.maximum(-lam, 0.0) + jnp.log(1.0 + jnp.exp(-jnp.abs(lam))))

    for hd in range(GROUPS):
        cols = slice(hd * hw, (hd + 1) * hw)
        xb = _dot(h_scr[...], win_ref[:, cols])
        xb_scr[pad:pad + tt, :, cols] = xb.reshape(tt, nb, hw)
        xc = cb_ref[:, cols][None] + xb_scr[0:tt, :, cols] * cw_ref[0:1, cols][None]
        for k in range(1, CONV_WIDTH):
            xc = xc + xb_scr[k:k + tt, :, cols] * cw_ref[k:k + 1, cols][None]
        xc2 = xc.reshape(tm, hw)
        xcb = xc2.astype(jnp.bfloat16)
        r = _sigmoid(_dot(xcb, gaw_ref[hd]) + gab_ref[:, cols])
        gi = _sigmoid(_dot(xcb, gxw_ref[hd]) + gxb_ref[:, cols])
        a = jnp.exp2(r * kl[:, cols])
        mult = jnp.sqrt(1.0 - a * a)
        bt = mult * (gi * xc2)
        a3 = a.reshape(tt, nb, hw)
        b3 = bt.reshape(tt, nb, hw)
        st = state_scr[:, cols]
        for t in range(tt):
            st = a3[t] * st + b3[t]
            hs_scr[t, :, cols] = st
        state_scr[:, cols] = st
        gt = _silu(_dot(h_scr[...], win_ref[:, e + hd * hw:e + (hd + 1) * hw]))
        y_scr[:, cols] = (hs_scr[:, :, cols].reshape(tm, hw) * gt).astype(jnp.bfloat16)

    xb_scr[0:pad] = xb_scr[tt:tt + pad]

    y = _dot(y_scr[...], wout_ref[...]).reshape(tt, nb, d)
    ms2 = jnp.mean(y * y, axis=-1, keepdims=True)
    o_ref[...] = x + y * jax.lax.rsqrt(ms2 + EPS) * (post_ref[...] * gate)[None]


def _layer_b(xt, mod_l, pre, post, w_in, conv_w, conv_b, ga_w, ga_b, gx_w, gx_b, lam, w_out,
             *, tt):
    s, nb, d = xt.shape
    e = w_out.shape[0]
    hw = e // GROUPS
    tm = tt * nb
    const2 = lambda i: (0, 0)
    const3 = lambda i: (0, 0, 0)
    single = pl.Buffered(1)
    kern = functools.partial(_layer_b_kernel, tt=tt, nb=nb, d=d, e=e)
    return pl.pallas_call(
        kern,
        grid=(s // tt,),
        in_specs=[
            pl.BlockSpec((tt, nb, d), lambda i: (i, 0, 0)),
            pl.BlockSpec((nb, 3 * d), const2, pipeline_mode=single),
            pl.BlockSpec((1, d), const2, pipeline_mode=single),
            pl.BlockSpec((1, d), const2, pipeline_mode=single),
            pl.BlockSpec((d, 2 * e), const2, pipeline_mode=single),
            pl.BlockSpec((CONV_WIDTH, e), const2, pipeline_mode=single),
            pl.BlockSpec((1, e), const2, pipeline_mode=single),
            pl.BlockSpec((GROUPS, hw, hw), const3, pipeline_mode=single),
            pl.BlockSpec((1, e), const2, pipeline_mode=single),
            pl.BlockSpec((GROUPS, hw, hw), const3, pipeline_mode=single),
            pl.BlockSpec((1, e), const2, pipeline_mode=single),
            pl.BlockSpec((1, e), const2, pipeline_mode=single),
            pl.BlockSpec((e, d), const2, pipeline_mode=single),
        ],
        out_specs=pl.BlockSpec((tt, nb, d), lambda i: (i, 0, 0)),
        out_shape=jax.ShapeDtypeStruct((s, nb, d), jnp.float32),
        scratch_shapes=[
            pltpu.VMEM((tm, d), jnp.bfloat16),
            pltpu.VMEM((tt + CONV_WIDTH - 1, nb, e), jnp.float32),
            pltpu.VMEM((tt, nb, e), jnp.float32),
            pltpu.VMEM((tm, e), jnp.bfloat16),
            pltpu.VMEM((nb, e), jnp.float32),
        ],
        compiler_params=pltpu.CompilerParams(
            dimension_semantics=("arbitrary",),
            vmem_limit_bytes=VMEM_LIMIT_BYTES),
        name="rglru_layer",
    )(xt, mod_l, pre.reshape(1, d), post.reshape(1, d), w_in.astype(jnp.bfloat16),
      conv_w, conv_b.reshape(1, e), ga_w.astype(jnp.bfloat16), ga_b.reshape(1, e),
      gx_w.astype(jnp.bfloat16), gx_b.reshape(1, e), lam.reshape(1, e),
      w_out.astype(jnp.bfloat16))


def kernel(x, c, mod_w, mod_b, pre_norm, post_norm, a_w_in, a_v_norm, a_w_s, a_b_s, a_w_out,
           b_w_in, b_conv_w, b_conv_b, b_ga_w, b_ga_b, b_gx_w, b_gx_b, b_lambda, b_w_out):
    depth = mod_w.shape[0]
    mod = _mod_call(c, mod_w, mod_b)
    for layer in range(depth):
        j = layer // 2
        if layer % 2 == 0:
            x = _layer_a(x, mod[layer], pre_norm[layer], post_norm[layer], a_w_in[j],
                         a_v_norm[j], a_w_s[j], a_b_s[j], a_w_out[j], tm=256)
        else:
            xt = jnp.transpose(x, (1, 0, 2))
            xt = _layer_b(xt, mod[layer], pre_norm[layer], post_norm[layer], b_w_in[j],
                          b_conv_w[j], b_conv_b[j], b_ga_w[j], b_ga_b[j], b_gx_w[j],
                          b_gx_b[j], b_lambda[j], b_w_out[j], tt=32)
            x = jnp.transpose(xt, (1, 0, 2))
    return x
```

```python
import functools
import math

import jax
import jax.numpy as jnp
from jax.experimental import pallas as pl
from jax.experimental.pallas import tpu as pltpu

EPS = 1e-6
CHUNK = 128
GROUPS = 8
CONV_WIDTH = 4
LRU_C = 8.0
MXU_N = 256
VMEM_LIMIT_BYTES = 56 * 1024 * 1024

_GELU_C0 = math.sqrt(2.0 / math.pi)
_GELU_C1 = 0.044715
_LOG2E = 1.0 / math.log(2.0)


def _gelu(x):
    inner = x * (x * x * (_GELU_C0 * _GELU_C1) + _GELU_C0)
    hx = 0.5 * x
    return hx * jnp.tanh(inner) + hx


def _silu(x):
    hx = 0.5 * x
    return hx * jnp.tanh(hx) + hx


def _sigmoid(x):
    return 0.5 * jnp.tanh(0.5 * x) + 0.5


def _col_tiles(w):
    k, n = w.shape
    return jnp.transpose(w.reshape(k, n // MXU_N, MXU_N), (1, 0, 2)).astype(jnp.bfloat16)


def _dot(a, b):
    return jnp.dot(a, b, preferred_element_type=jnp.float32)


def _mod_kernel(c_ref, w_ref, b_ref, o_ref):
    c = c_ref[...]
    cond = c * _sigmoid(c)
    o_ref[...] = _dot(cond.astype(jnp.bfloat16), w_ref[...].astype(jnp.bfloat16)) + b_ref[...]


def _mod_call(c, mod_w, mod_b):
    depth, d, d3 = mod_w.shape
    b = c.shape[0]
    nb = d3 // d
    return pl.pallas_call(
        _mod_kernel,
        grid=(depth, nb),
        in_specs=[
            pl.BlockSpec((b, d), lambda l, n: (0, 0)),
            pl.BlockSpec((None, d, d), lambda l, n: (l, 0, n)),
            pl.BlockSpec((None, 1, d), lambda l, n: (l, 0, n)),
        ],
        out_specs=pl.BlockSpec((None, b, d), lambda l, n: (l, 0, n)),
        out_shape=jax.ShapeDtypeStruct((depth, b, d3), jnp.float32),
        compiler_params=pltpu.CompilerParams(
            dimension_semantics=("arbitrary", "arbitrary"),
            vmem_limit_bytes=VMEM_LIMIT_BYTES),
        name="adaln_mod",
    )(c, mod_w, mod_b.reshape(depth, 1, d3))


def _layer_a_kernel(x_ref, mod_ref, pre_ref, post_ref, win_ref, vnorm_ref, ws_ref, bs_ref,
                    wout_ref, o_ref, h_scr, v_scr, y_scr, wm_scr, *, tm, d, e):
    gw = e // GROUPS

    @pl.when((pl.program_id(0) == 0) & (pl.program_id(1) == 0))
    def _():
        row = jax.lax.broadcasted_iota(jnp.int32, (CHUNK, CHUNK), 0)
        col = jax.lax.broadcasted_iota(jnp.int32, (CHUNK, CHUNK), 1)
        for g in range(GROUPS):
            wm_scr[g] = jnp.where(col <= row, ws_ref[g], 0.0)

    x = x_ref[...]
    mod = mod_ref[...]
    shift, scale, gate = mod[:, :d], mod[:, d:2 * d], mod[:, 2 * d:]
    ms = jnp.mean(x * x, axis=-1, keepdims=True)
    h = x * jax.lax.rsqrt(ms + EPS) * (pre_ref[...] * (1.0 + scale)) + shift
    h_scr[...] = h

    s1 = jnp.zeros((tm, 1), jnp.float32)
    s2 = jnp.zeros((tm, 1), jnp.float32)
    for g in range(GROUPS):
        vc = _gelu(_dot(h_scr[...], win_ref[GROUPS + g]))
        v_scr[:, g * gw:(g + 1) * gw] = vc
        s1 = s1 + jnp.sum(vc, axis=-1, keepdims=True)
        s2 = s2 + jnp.sum(vc * vc, axis=-1, keepdims=True)
    mean = s1 * (1.0 / e)
    var = s2 * (1.0 / e) - mean * mean
    rstd = jax.lax.rsqrt(var + EPS)

    for g in range(GROUPS):
        cols = slice(g * gw, (g + 1) * gw)
        vn = ((v_scr[:, cols] - mean) * rstd * vnorm_ref[:, cols]).astype(jnp.bfloat16)
        bias = bs_ref[g]
        bias = jnp.concatenate([bias] * (gw // 128), axis=1)
        parts = []
        for j in range(tm // CHUNK):
            mx = _dot(wm_scr[g], vn[j * CHUNK:(j + 1) * CHUNK, :]) + bias
            parts.append(mx)
        mixed = jnp.concatenate(parts, axis=0) if len(parts) > 1 else parts[0]
        u = _gelu(_dot(h_scr[...], win_ref[g]))
        gt = _silu(_dot(h_scr[...], win_ref[2 * GROUPS + g]))
        y_scr[:, cols] = u * mixed * gt

    y = jnp.concatenate([_dot(y_scr[...], wout_ref[n]) for n in range(d // MXU_N)], axis=1)
    ms2 = jnp.mean(y * y, axis=-1, keepdims=True)
    o_ref[...] = x + y * jax.lax.rsqrt(ms2 + EPS) * (post_ref[...] * gate)


def _layer_a(x, mod_l, pre, post, w_in, v_norm, w_s, b_s, w_out, *, tm):
    b, s, d = x.shape
    e = w_out.shape[0]
    const2 = lambda bi, i: (0, 0)
    const3 = lambda bi, i: (0, 0, 0)
    single = pl.Buffered(1)
    kern = functools.partial(_layer_a_kernel, tm=tm, d=d, e=e)
    return pl.pallas_call(
        kern,
        grid=(b, s // tm),
        in_specs=[
            pl.BlockSpec((None, tm, d), lambda bi, i: (bi, i, 0)),
            pl.BlockSpec((None, 1, 3 * d), lambda bi, i: (bi, 0, 0)),
            pl.BlockSpec((1, d), const2, pipeline_mode=single),
            pl.BlockSpec((1, d), const2, pipeline_mode=single),
            pl.BlockSpec((3 * e // MXU_N, d, MXU_N), const3, pipeline_mode=single),
            pl.BlockSpec((1, e), const2, pipeline_mode=single),
            pl.BlockSpec((GROUPS, CHUNK, CHUNK), const3, pipeline_mode=single),
            pl.BlockSpec((GROUPS, CHUNK, 128), const3, pipeline_mode=single),
            pl.BlockSpec((d // MXU_N, e, MXU_N), const3, pipeline_mode=single),
        ],
        out_specs=pl.BlockSpec((None, tm, d), lambda bi, i: (bi, i, 0)),
        out_shape=jax.ShapeDtypeStruct((b, s, d), jnp.float32),
        scratch_shapes=[
            pltpu.VMEM((tm, d), jnp.float32),
            pltpu.VMEM((tm, e), jnp.float32),
            pltpu.VMEM((tm, e), jnp.float32),
            pltpu.VMEM((GROUPS, CHUNK, CHUNK), jnp.float32),
        ],
        compiler_params=pltpu.CompilerParams(
            dimension_semantics=("arbitrary", "arbitrary"),
            vmem_limit_bytes=VMEM_LIMIT_BYTES),
        name="gmlp_layer",
    )(x, mod_l.reshape(b, 1, 3 * d), pre.reshape(1, d), post.reshape(1, d),
      _col_tiles(w_in), v_norm.reshape(1, e), w_s,
      jnp.broadcast_to(b_s[:, :, None], (GROUPS, CHUNK, 128)),
      _col_tiles(w_out))


def _x_copy(x_hbm, xbuf, sem, step, slot, b, tt):
    return pltpu.make_async_copy(x_hbm.at[b, pl.ds(step * tt, tt), :], xbuf.at[slot, :, b, :],
                                 sem.at[slot, b])


def _o_copy(obuf, o_hbm, sem, step, slot, b, tt):
    return pltpu.make_async_copy(obuf.at[slot, :, b, :], o_hbm.at[b, pl.ds(step * tt, tt), :],
                                 sem.at[slot, b])


def _layer_b_kernel(x_hbm, mod_ref, pre_ref, post_ref, win_ref, cw_ref, cb_ref, gaw_ref,
                    gab_ref, gxw_ref, gxb_ref, lam_ref, wout_ref, o_hbm,
                    xbuf, obuf, sem_in, sem_out, h_scr, xb_scr, hs_scr, y_scr, state_scr,
                    *, tt, nb, d, e, nsteps):
    hw = e // GROUPS
    tm = tt * nb
    pad = CONV_WIDTH - 1
    i = pl.program_id(0)
    slot = i % 2

    @pl.when(i == 0)
    def _():
        xb_scr[0:pad] = jnp.zeros((pad, nb, e), jnp.float32)
        state_scr[...] = jnp.zeros((nb, e), jnp.float32)
        for b in range(nb):
            _x_copy(x_hbm, xbuf, sem_in, 0, 0, b, tt).start()

    @pl.when(i + 1 < nsteps)
    def _():
        for b in range(nb):
            _x_copy(x_hbm, xbuf, sem_in, i + 1, 1 - slot, b, tt).start()

    for b in range(nb):
        _x_copy(x_hbm, xbuf, sem_in, i, slot, b, tt).wait()

    @pl.when(i >= 2)
    def _():
        for b in range(nb):
            _o_copy(obuf, o_hbm, sem_out, i - 2, slot, b, tt).wait()

    x = xbuf[slot]
    mod = mod_ref[...]
    shift, scale, gate = mod[:, :d], mod[:, d:2 * d], mod[:, 2 * d:]
    ms = jnp.mean(x * x, axis=-1, keepdims=True)
    h = x * jax.lax.rsqrt(ms + EPS) * (pre_ref[...] * (1.0 + scale))[None] + shift[None]
    h_scr[...] = h.reshape(tm, d)

    lam = lam_ref[...]
    hkl = (-0.5 * LRU_C * _LOG2E) * (jnp.maximum(-lam, 0.0) + jnp.log(1.0 + jnp.exp(-jnp.abs(lam))))
    hcw = 0.5 * cw_ref[...]
    hcb = 0.5 * cb_ref[...]
    hgab = 0.5 * gab_ref[...]
    hgxb = 0.5 * gxb_ref[...]

    for hd in range(GROUPS):
        cols = slice(hd * hw, (hd + 1) * hw)
        xb = _dot(h_scr[...], win_ref[hd])
        xb_scr[pad:pad + tt, :, cols] = xb.reshape(tt, nb, hw)
        hxc = hcb[:, cols][None] + xb_scr[0:tt, :, cols] * hcw[0:1, cols][None]
        for k in range(1, CONV_WIDTH):
            hxc = hxc + xb_scr[k:k + tt, :, cols] * hcw[k:k + 1, cols][None]
        hxc = hxc.reshape(tm, hw)
        tr = jnp.tanh(_dot(hxc, gaw_ref[hd]) + hgab[:, cols])
        ti = jnp.tanh(_dot(hxc, gxw_ref[hd]) + hgxb[:, cols])
        a = jnp.exp2(tr * hkl[:, cols] + hkl[:, cols])
        q = 1.0 - a * a
        mult = q * jax.lax.rsqrt(jnp.maximum(q, 1e-30))
        bt = mult * (ti * hxc + hxc)
        a3 = a.reshape(tt, nb, hw)
        b3 = bt.reshape(tt, nb, hw)
        st = state_scr[:, cols]
        for t in range(tt):
            st = a3[t] * st + b3[t]
            hs_scr[t, :, cols] = st
        state_scr[:, cols] = st
        gt = _silu(_dot(h_scr[...], win_ref[GROUPS + hd]))
        y_scr[:, cols] = hs_scr[:, :, cols].reshape(tm, hw) * gt

    xb_scr[0:pad] = xb_scr[tt:tt + pad]

    y = jnp.concatenate([_dot(y_scr[...], wout_ref[n]) for n in range(d // MXU_N)], axis=1)
    y = y.reshape(tt, nb, d)
    ms2 = jnp.mean(y * y, axis=-1, keepdims=True)
    obuf[slot] = x + y * jax.lax.rsqrt(ms2 + EPS) * (post_ref[...] * gate)[None]
    for b in range(nb):
        _o_copy(obuf, o_hbm, sem_out, i, slot, b, tt).start()

    @pl.when(i == nsteps - 1)
    def _():
        for b in range(nb):
            _o_copy(obuf, o_hbm, sem_out, i, slot, b, tt).wait()
        if nsteps >= 2:
            for b in range(nb):
                _o_copy(obuf, o_hbm, sem_out, i - 1, 1 - slot, b, tt).wait()


def _layer_b(x, mod_l, pre, post, w_in, conv_w, conv_b, ga_w, ga_b, gx_w, gx_b, lam, w_out,
             *, tt):
    nb, s, d = x.shape
    e = w_out.shape[0]
    hw = e // GROUPS
    tm = tt * nb
    const2 = lambda i: (0, 0)
    const3 = lambda i: (0, 0, 0)
    single = pl.Buffered(1)
    nsteps = s // tt
    kern = functools.partial(_layer_b_kernel, tt=tt, nb=nb, d=d, e=e, nsteps=nsteps)
    return pl.pallas_call(
        kern,
        grid=(nsteps,),
        in_specs=[
            pl.BlockSpec(memory_space=pl.ANY),
            pl.BlockSpec((nb, 3 * d), const2, pipeline_mode=single),
            pl.BlockSpec((1, d), const2, pipeline_mode=single),
            pl.BlockSpec((1, d), const2, pipeline_mode=single),
            pl.BlockSpec((2 * e // MXU_N, d, MXU_N), const3, pipeline_mode=single),
            pl.BlockSpec((CONV_WIDTH, e), const2, pipeline_mode=single),
            pl.BlockSpec((1, e), const2, pipeline_mode=single),
            pl.BlockSpec((GROUPS, hw, hw), const3, pipeline_mode=single),
            pl.BlockSpec((1, e), const2, pipeline_mode=single),
            pl.BlockSpec((GROUPS, hw, hw), const3, pipeline_mode=single),
            pl.BlockSpec((1, e), const2, pipeline_mode=single),
            pl.BlockSpec((1, e), const2, pipeline_mode=single),
            pl.BlockSpec((d // MXU_N, e, MXU_N), const3, pipeline_mode=single),
        ],
        out_specs=pl.BlockSpec(memory_space=pl.ANY),
        out_shape=jax.ShapeDtypeStruct((nb, s, d), jnp.float32),
        scratch_shapes=[
            pltpu.VMEM((2, tt, nb, d), jnp.float32),
            pltpu.VMEM((2, tt, nb, d), jnp.float32),
            pltpu.SemaphoreType.DMA((2, nb)),
            pltpu.SemaphoreType.DMA((2, nb)),
            pltpu.VMEM((tm, d), jnp.float32),
            pltpu.VMEM((tt + CONV_WIDTH - 1, nb, e), jnp.float32),
            pltpu.VMEM((tt, nb, e), jnp.float32),
            pltpu.VMEM((tm, e), jnp.float32),
            pltpu.VMEM((nb, e), jnp.float32),
        ],
        compiler_params=pltpu.CompilerParams(
            dimension_semantics=("arbitrary",),
            vmem_limit_bytes=VMEM_LIMIT_BYTES),
        name="rglru_layer",
    )(x, mod_l, pre.reshape(1, d), post.reshape(1, d), _col_tiles(w_in),
      conv_w, conv_b.reshape(1, e), ga_w.astype(jnp.bfloat16), ga_b.reshape(1, e),
      gx_w.astype(jnp.bfloat16), gx_b.reshape(1, e), lam.reshape(1, e),
      _col_tiles(w_out))


def kernel(x, c, mod_w, mod_b, pre_norm, post_norm, a_w_in, a_v_norm, a_w_s, a_b_s, a_w_out,
           b_w_in, b_conv_w, b_conv_b, b_ga_w, b_ga_b, b_gx_w, b_gx_b, b_lambda, b_w_out):
    depth = mod_w.shape[0]
    mod = _mod_call(c, mod_w, mod_b)
    for layer in range(depth):
        j = layer // 2
        if layer % 2 == 0:
            x = _layer_a(x, mod[layer], pre_norm[layer], post_norm[layer], a_w_in[j],
                         a_v_norm[j], a_w_s[j], a_b_s[j], a_w_out[j], tm=512)
        else:
            x = _layer_b(x, mod[layer], pre_norm[layer], post_norm[layer], b_w_in[j],
                         b_conv_w[j], b_conv_b[j], b_ga_w[j], b_ga_b[j], b_gx_w[j],
                         b_gx_b[j], b_lambda[j], b_w_out[j], tt=64)
    return x
```

```python
import functools
import math

import jax
import jax.numpy as jnp
from jax.experimental import pallas as pl
from jax.experimental.pallas import tpu as pltpu

EPS = 1e-6
CHUNK = 128
GROUPS = 8
CONV_WIDTH = 4
LRU_C = 8.0
MXU_N = 256
VMEM_LIMIT_BYTES = 56 * 1024 * 1024

_GELU_C0 = math.sqrt(2.0 / math.pi)
_GELU_C1 = 0.044715
_LOG2E = 1.0 / math.log(2.0)


def _gelu(x):
    inner = x * (x * x * (_GELU_C0 * _GELU_C1) + _GELU_C0)
    hx = 0.5 * x
    return hx * jnp.tanh(inner) + hx


def _silu(x):
    hx = 0.5 * x
    return hx * jnp.tanh(hx) + hx


def _sigmoid(x):
    return 0.5 * jnp.tanh(0.5 * x) + 0.5


def _dot(a, b):
    return jnp.dot(a, b, preferred_element_type=jnp.float32)


def _weight_copies(w_hbm, widx, w_scr, sem, sem_base):
    n_tiles = w_scr.shape[0]
    return [pltpu.make_async_copy(w_hbm.at[widx, :, pl.ds(n * MXU_N, MXU_N)], w_scr.at[n],
                                  sem.at[sem_base + n]) for n in range(n_tiles)]


def _mod_kernel(c_ref, w_ref, b_ref, o_ref):
    c = c_ref[...]
    cond = c * _sigmoid(c)
    o_ref[...] = _dot(cond.astype(jnp.bfloat16), w_ref[...].astype(jnp.bfloat16)) + b_ref[...]


def _mod_call(c, mod_w, mod_b):
    depth, d, d3 = mod_w.shape
    b = c.shape[0]
    nb = d3 // d
    return pl.pallas_call(
        _mod_kernel,
        grid=(depth, nb),
        in_specs=[
            pl.BlockSpec((b, d), lambda l, n: (0, 0)),
            pl.BlockSpec((None, d, d), lambda l, n: (l, 0, n)),
            pl.BlockSpec((None, 1, d), lambda l, n: (l, 0, n)),
        ],
        out_specs=pl.BlockSpec((None, b, d), lambda l, n: (l, 0, n)),
        out_shape=jax.ShapeDtypeStruct((depth, b, d3), jnp.float32),
        compiler_params=pltpu.CompilerParams(
            dimension_semantics=("arbitrary", "arbitrary"),
            vmem_limit_bytes=VMEM_LIMIT_BYTES),
        name="adaln_mod",
    )(c, mod_w, mod_b.reshape(depth, 1, d3))


def _layer_a_kernel(xc_ref, xp_ref, modc_ref, modp_ref, pre_ref, post_ref, win_hbm, vnorm_ref,
                    ws_ref, bs_ref, wout_hbm, o_ref,
                    win_scr, wout_scr, wsem, h_scr, v_scr, y_scr, wm_scr,
                    *, tm, d, e, nsteps, widx):
    gw = e // GROUPS
    n_out = d // MXU_N
    i = pl.program_id(0)

    @pl.when(i == 0)
    def _():
        copies = (_weight_copies(win_hbm, widx, win_scr, wsem, 0)
                  + _weight_copies(wout_hbm, widx, wout_scr, wsem, win_scr.shape[0]))
        for cp in copies:
            cp.start()
        row = jax.lax.broadcasted_iota(jnp.int32, (CHUNK, CHUNK), 0)
        col = jax.lax.broadcasted_iota(jnp.int32, (CHUNK, CHUNK), 1)
        for g in range(GROUPS):
            wm_scr[g] = jnp.where(col <= row, ws_ref[g], 0.0)
        y_scr[...] = jnp.zeros((tm, e), jnp.float32)
        for cp in copies:
            cp.wait()

    def out_proj():
        return jnp.concatenate([_dot(y_scr[...], wout_scr[n]) for n in range(n_out)], axis=1)

    def out_stage(y):
        gate = modp_ref[:, 2 * d:]
        ms2 = jnp.mean(y * y, axis=-1, keepdims=True)
        o_ref[...] = xp_ref[...] + y * jax.lax.rsqrt(ms2 + EPS) * (post_ref[...] * gate)

    @pl.when(i < nsteps)
    def _():
        y_prev = out_proj()

        x = xc_ref[...]
        shift, scale = modc_ref[:, :d], modc_ref[:, d:2 * d]
        ms = jnp.mean(x * x, axis=-1, keepdims=True)
        h_scr[...] = x * jax.lax.rsqrt(ms + EPS) * (pre_ref[...] * (1.0 + scale)) + shift

        v_raw = _dot(h_scr[...], win_scr[GROUPS])
        out_stage(y_prev)

        s1 = jnp.zeros((tm, 1), jnp.float32)
        s2 = jnp.zeros((tm, 1), jnp.float32)
        for g in range(GROUPS):
            if g + 1 < GROUPS:
                nxt = _dot(h_scr[...], win_scr[GROUPS + g + 1])
            else:
                u_raw = _dot(h_scr[...], win_scr[0])
                g_raw = _dot(h_scr[...], win_scr[2 * GROUPS])
            vc = _gelu(v_raw)
            v_scr[:, g * gw:(g + 1) * gw] = vc
            s1 = s1 + jnp.sum(vc, axis=-1, keepdims=True)
            s2 = s2 + jnp.sum(vc * vc, axis=-1, keepdims=True)
            if g + 1 < GROUPS:
                v_raw = nxt
        mean = s1 * (1.0 / e)
        var = s2 * (1.0 / e) - mean * mean
        rstd = jax.lax.rsqrt(var + EPS)

        for g in range(GROUPS):
            cols = slice(g * gw, (g + 1) * gw)
            vn = ((v_scr[:, cols] - mean) * rstd * vnorm_ref[:, cols]).astype(jnp.bfloat16)
            bias = bs_ref[g]
            bias = jnp.concatenate([bias] * (gw // 128), axis=1)
            parts = [_dot(wm_scr[g], vn[j * CHUNK:(j + 1) * CHUNK, :]) + bias
                     for j in range(tm // CHUNK)]
            mixed = jnp.concatenate(parts, axis=0)
            if g + 1 < GROUPS:
                u_nxt = _dot(h_scr[...], win_scr[g + 1])
                g_nxt = _dot(h_scr[...], win_scr[2 * GROUPS + g + 1])
            y_scr[:, cols] = _gelu(u_raw) * mixed * _silu(g_raw)
            if g + 1 < GROUPS:
                u_raw, g_raw = u_nxt, g_nxt

    @pl.when(i == nsteps)
    def _():
        out_stage(out_proj())


def _layer_a(x, mod, layer, widx, pre, post, w_in, v_norm, w_s, bs_lanes, w_out, *, tm):
    b, s, d = x.shape
    e = w_out.shape[1]
    depth = mod.shape[0]
    tpb = s // tm
    nsteps = b * tpb
    cur = lambda i: jnp.minimum(i, nsteps - 1)
    prev = lambda i: jnp.maximum(i - 1, 0)
    single = pl.Buffered(1)
    kern = functools.partial(_layer_a_kernel, tm=tm, d=d, e=e, nsteps=nsteps, widx=widx)
    out = pl.pallas_call(
        kern,
        grid=(nsteps + 1,),
        in_specs=[
            pl.BlockSpec((tm, d), lambda i: (cur(i), 0)),
            pl.BlockSpec((tm, d), lambda i: (prev(i), 0)),
            pl.BlockSpec((None, None, 1, 3 * d), lambda i: (layer, cur(i) // tpb, 0, 0)),
            pl.BlockSpec((None, None, 1, 3 * d), lambda i: (layer, prev(i) // tpb, 0, 0)),
            pl.BlockSpec((None, 1, d), lambda i: (layer, 0, 0), pipeline_mode=single),
            pl.BlockSpec((None, 1, d), lambda i: (layer, 0, 0), pipeline_mode=single),
            pl.BlockSpec(memory_space=pl.ANY),
            pl.BlockSpec((None, 1, e), lambda i: (widx, 0, 0), pipeline_mode=single),
            pl.BlockSpec((None, GROUPS, CHUNK, CHUNK), lambda i: (widx, 0, 0, 0),
                         pipeline_mode=single),
            pl.BlockSpec((None, GROUPS, CHUNK, 128), lambda i: (widx, 0, 0, 0),
                         pipeline_mode=single),
            pl.BlockSpec(memory_space=pl.ANY),
        ],
        out_specs=pl.BlockSpec((tm, d), lambda i: (prev(i), 0)),
        out_shape=jax.ShapeDtypeStruct((b * s, d), jnp.float32),
        scratch_shapes=[
            pltpu.VMEM((3 * e // MXU_N, d, MXU_N), jnp.bfloat16),
            pltpu.VMEM((d // MXU_N, e, MXU_N), jnp.bfloat16),
            pltpu.SemaphoreType.DMA((3 * e // MXU_N + d // MXU_N,)),
            pltpu.VMEM((tm, d), jnp.float32),
            pltpu.VMEM((tm, e), jnp.float32),
            pltpu.VMEM((tm, e), jnp.float32),
            pltpu.VMEM((GROUPS, CHUNK, CHUNK), jnp.float32),
        ],
        compiler_params=pltpu.CompilerParams(
            dimension_semantics=("arbitrary",),
            vmem_limit_bytes=VMEM_LIMIT_BYTES),
        name="gmlp_layer",
    )(x.reshape(b * s, d), x.reshape(b * s, d), mod.reshape(depth, b, 1, 3 * d),
      mod.reshape(depth, b, 1, 3 * d), pre, post, w_in, v_norm, w_s, bs_lanes, w_out)
    return out.reshape(b, s, d)


X_SLOTS = 3
O_SLOTS = 2


def _x_copy(x_hbm, xbuf, sem, step, slot, b, tt):
    return pltpu.make_async_copy(x_hbm.at[b, pl.ds(step * tt, tt), :], xbuf.at[slot, :, b, :],
                                 sem.at[slot, b])


def _o_copy(obuf, o_hbm, sem, step, slot, b, tt):
    return pltpu.make_async_copy(obuf.at[slot, :, b, :], o_hbm.at[b, pl.ds(step * tt, tt), :],
                                 sem.at[slot, b])


def _layer_b_kernel(x_hbm, mod_ref, pre_ref, post_ref, win_hbm, cw_ref, cb_ref, gaw_ref,
                    gab_ref, gxw_ref, gxb_ref, lam_ref, wout_hbm, o_hbm,
                    xbuf, obuf, sem_in, sem_out, win_scr, wout_scr, wsem,
                    h_scr, xb_scr, hs_scr, y_scr, state_scr,
                    *, tt, nb, d, e, nsteps, widx):
    hw = e // GROUPS
    tm = tt * nb
    pad = CONV_WIDTH - 1
    n_out = d // MXU_N
    i = pl.program_id(0)

    @pl.when(i == 0)
    def _():
        copies = (_weight_copies(win_hbm, widx, win_scr, wsem, 0)
                  + _weight_copies(wout_hbm, widx, wout_scr, wsem, win_scr.shape[0]))
        for cp in copies:
            cp.start()
        for b in range(nb):
            _x_copy(x_hbm, xbuf, sem_in, 0, 0, b, tt).start()
        xb_scr[0:pad] = jnp.zeros((pad, nb, e), jnp.float32)
        state_scr[...] = jnp.zeros((nb, e), jnp.float32)
        y_scr[...] = jnp.zeros((tm, e), jnp.float32)
        xbuf[X_SLOTS - 1] = jnp.zeros((tt, nb, d), jnp.float32)
        for cp in copies:
            cp.wait()

    @pl.when(i + 1 < nsteps)
    def _():
        for b in range(nb):
            _x_copy(x_hbm, xbuf, sem_in, i + 1, (i + 1) % X_SLOTS, b, tt).start()

    @pl.when(i < nsteps)
    def _():
        for b in range(nb):
            _x_copy(x_hbm, xbuf, sem_in, i, i % X_SLOTS, b, tt).wait()

    @pl.when(i >= O_SLOTS + 1)
    def _():
        for b in range(nb):
            _o_copy(obuf, o_hbm, sem_out, i - 1 - O_SLOTS, (i - 1) % O_SLOTS, b, tt).wait()

    def out_proj():
        return jnp.concatenate([_dot(y_scr[...], wout_scr[n]) for n in range(n_out)], axis=1)

    def out_stage(y):
        gate = mod_ref[:, 2 * d:]
        xp = xbuf[(i + X_SLOTS - 1) % X_SLOTS]
        y = y.reshape(tt, nb, d)
        ms2 = jnp.mean(y * y, axis=-1, keepdims=True)
        obuf[(i + O_SLOTS - 1) % O_SLOTS] = (
            xp + y * jax.lax.rsqrt(ms2 + EPS) * (post_ref[...] * gate)[None])

    @pl.when(i < nsteps)
    def _():
        y_prev = out_proj()

        x = xbuf[i % X_SLOTS]
        shift, scale = mod_ref[:, :d], mod_ref[:, d:2 * d]
        ms = jnp.mean(x * x, axis=-1, keepdims=True)
        h = x * jax.lax.rsqrt(ms + EPS) * (pre_ref[...] * (1.0 + scale))[None] + shift[None]
        h_scr[...] = h.reshape(tm, d)

        xb_scr[pad:pad + tt, :, 0:hw] = _dot(h_scr[...], win_scr[0]).reshape(tt, nb, hw)
        out_stage(y_prev)

        lam = lam_ref[...]
        hkl = (-0.5 * LRU_C * _LOG2E) * (
            jnp.maximum(-lam, 0.0) + jnp.log(1.0 + jnp.exp(-jnp.abs(lam))))
        hcw = 0.5 * cw_ref[...]
        hcb = 0.5 * cb_ref[...]
        hgab = 0.5 * gab_ref[...]
        hgxb = 0.5 * gxb_ref[...]

        for hd in range(GROUPS):
            cols = slice(hd * hw, (hd + 1) * hw)
            if hd + 1 < GROUPS:
                ncols = slice((hd + 1) * hw, (hd + 2) * hw)
                xb_scr[pad:pad + tt, :, ncols] = (
                    _dot(h_scr[...], win_scr[hd + 1]).reshape(tt, nb, hw))
            hxc = hcb[:, cols][None] + xb_scr[0:tt, :, cols] * hcw[0:1, cols][None]
            for k in range(1, CONV_WIDTH):
                hxc = hxc + xb_scr[k:k + tt, :, cols] * hcw[k:k + 1, cols][None]
            hxc = hxc.reshape(tm, hw)
            tr = jnp.tanh(_dot(hxc, gaw_ref[hd]) + hgab[:, cols])
            ti = jnp.tanh(_dot(hxc, gxw_ref[hd]) + hgxb[:, cols])
            g_raw = _dot(h_scr[...], win_scr[GROUPS + hd])
            a = jnp.exp2(tr * hkl[:, cols] + hkl[:, cols])
            q = 1.0 - a * a
            mult = q * jax.lax.rsqrt(jnp.maximum(q, 1e-30))
            bt = mult * (ti * hxc + hxc)
            a3 = a.reshape(tt, nb, hw)
            b3 = bt.reshape(tt, nb, hw)
            st = state_scr[:, cols]
            for t in range(tt):
                st = a3[t] * st + b3[t]
                hs_scr[t, :, cols] = st
            state_scr[:, cols] = st
            y_scr[:, cols] = hs_scr[:, :, cols].reshape(tm, hw) * _silu(g_raw)

        xb_scr[0:pad] = xb_scr[tt:tt + pad]

    @pl.when(i == nsteps)
    def _():
        out_stage(out_proj())

    @pl.when(i >= 1)
    def _():
        for b in range(nb):
            _o_copy(obuf, o_hbm, sem_out, i - 1, (i - 1) % O_SLOTS, b, tt).start()

    @pl.when(i == nsteps)
    def _():
        for k in range(min(O_SLOTS, nsteps)):
            for b in range(nb):
                _o_copy(obuf, o_hbm, sem_out, i - 1 - k, (i - 1 - k) % O_SLOTS, b, tt).wait()


def _layer_b(x, mod, layer, widx, pre, post, w_in, conv_w, conv_b, ga_w, ga_b, gx_w, gx_b, lam,
             w_out, *, tt):
    nb, s, d = x.shape
    e = w_out.shape[1]
    hw = e // GROUPS
    tm = tt * nb
    nsteps = s // tt
    single = pl.Buffered(1)
    by_layer = lambda i: (layer, 0, 0)
    by_w = lambda i: (widx, 0, 0)
    by_w4 = lambda i: (widx, 0, 0, 0)
    kern = functools.partial(_layer_b_kernel, tt=tt, nb=nb, d=d, e=e, nsteps=nsteps, widx=widx)
    return pl.pallas_call(
        kern,
        grid=(nsteps + 1,),
        in_specs=[
            pl.BlockSpec(memory_space=pl.ANY),
            pl.BlockSpec((None, nb, 3 * d), by_layer, pipeline_mode=single),
            pl.BlockSpec((None, 1, d), by_layer, pipeline_mode=single),
            pl.BlockSpec((None, 1, d), by_layer, pipeline_mode=single),
            pl.BlockSpec(memory_space=pl.ANY),
            pl.BlockSpec((None, CONV_WIDTH, e), by_w, pipeline_mode=single),
            pl.BlockSpec((None, 1, e), by_w, pipeline_mode=single),
            pl.BlockSpec((None, GROUPS, hw, hw), by_w4, pipeline_mode=single),
            pl.BlockSpec((None, 1, e), by_w, pipeline_mode=single),
            pl.BlockSpec((None, GROUPS, hw, hw), by_w4, pipeline_mode=single),
            pl.BlockSpec((None, 1, e), by_w, pipeline_mode=single),
            pl.BlockSpec((None, 1, e), by_w, pipeline_mode=single),
            pl.BlockSpec(memory_space=pl.ANY),
        ],
        out_specs=pl.BlockSpec(memory_space=pl.ANY),
        out_shape=jax.ShapeDtypeStruct((nb, s, d), jnp.float32),
        scratch_shapes=[
            pltpu.VMEM((X_SLOTS, tt, nb, d), jnp.float32),
            pltpu.VMEM((O_SLOTS, tt, nb, d), jnp.float32),
            pltpu.SemaphoreType.DMA((X_SLOTS, nb)),
            pltpu.SemaphoreType.DMA((O_SLOTS, nb)),
            pltpu.VMEM((2 * e // MXU_N, d, MXU_N), jnp.bfloat16),
            pltpu.VMEM((d // MXU_N, e, MXU_N), jnp.bfloat16),
            pltpu.SemaphoreType.DMA((2 * e // MXU_N + d // MXU_N,)),
            pltpu.VMEM((tm, d), jnp.float32),
            pltpu.VMEM((tt + CONV_WIDTH - 1, nb, e), jnp.float32),
            pltpu.VMEM((tt, nb, e), jnp.float32),
            pltpu.VMEM((tm, e), jnp.float32),
            pltpu.VMEM((nb, e), jnp.float32),
        ],
        compiler_params=pltpu.CompilerParams(
            dimension_semantics=("arbitrary",),
            vmem_limit_bytes=VMEM_LIMIT_BYTES),
        name="rglru_layer",
    )(x, mod, pre, post, w_in, conv_w, conv_b, ga_w, ga_b, gx_w, gx_b, lam, w_out)


def kernel(x, c, mod_w, mod_b, pre_norm, post_norm, a_w_in, a_v_norm, a_w_s, a_b_s, a_w_out,
           b_w_in, b_conv_w, b_conv_b, b_ga_w, b_ga_b, b_gx_w, b_gx_b, b_lambda, b_w_out):
    depth, d = pre_norm.shape
    n_a, e = a_v_norm.shape
    n_b = b_lambda.shape[0]
    bf16 = jnp.bfloat16
    mod = _mod_call(c, mod_w, mod_b)
    pre = pre_norm.reshape(depth, 1, d)
    post = post_norm.reshape(depth, 1, d)
    a_w_in, a_w_out = a_w_in.astype(bf16), a_w_out.astype(bf16)
    b_w_in, b_w_out = b_w_in.astype(bf16), b_w_out.astype(bf16)
    b_ga_w, b_gx_w = b_ga_w.astype(bf16), b_gx_w.astype(bf16)
    a_v_norm = a_v_norm.reshape(n_a, 1, e)
    a_bs_lanes = jnp.broadcast_to(a_b_s[..., None], a_b_s.shape + (128,))
    b_conv_b, b_ga_b, b_gx_b, b_lambda = (
        p.reshape(n_b, 1, e) for p in (b_conv_b, b_ga_b, b_gx_b, b_lambda))
    for layer in range(depth):
        j = layer // 2
        if layer % 2 == 0:
            x = _layer_a(x, mod, layer, j, pre, post, a_w_in, a_v_norm, a_w_s, a_bs_lanes,
                         a_w_out, tm=512)
        else:
            x = _layer_b(x, mod, layer, j, pre, post, b_w_in, b_conv_w, b_conv_b, b_ga_w,
                         b_ga_b, b_gx_w, b_gx_b, b_lambda, b_w_out, tt=64)
    return x
```

```python
import functools
import math

import jax
import jax.numpy as jnp
from jax.experimental import pallas as pl
from jax.experimental.pallas import tpu as pltpu

EPS = 1e-6
CHUNK = 128
GROUPS = 8
CONV_WIDTH = 4
LRU_C = 8.0
MXU_N = 256
VMEM_LIMIT_BYTES = 56 * 1024 * 1024

_GELU_C0 = math.sqrt(2.0 / math.pi)
_GELU_C1 = 0.044715
_LOG2E = 1.0 / math.log(2.0)


def _gelu(x):
    inner = x * (x * x * (_GELU_C0 * _GELU_C1) + _GELU_C0)
    hx = 0.5 * x
    return hx * jnp.tanh(inner) + hx


def _silu(x):
    hx = 0.5 * x
    return hx * jnp.tanh(hx) + hx


def _sigmoid(x):
    return 0.5 * jnp.tanh(0.5 * x) + 0.5


def _dot(a, b):
    return jnp.dot(a, b, preferred_element_type=jnp.float32)


def _weight_copies(w_hbm, widx, w_scr, sem, sem_base):
    n_tiles = w_scr.shape[0]
    return [pltpu.make_async_copy(w_hbm.at[widx, :, pl.ds(n * MXU_N, MXU_N)], w_scr.at[n],
                                  sem.at[sem_base + n]) for n in range(n_tiles)]


def _mod_kernel(c_ref, w_ref, b_ref, o_ref):
    c = c_ref[...]
    cond = c * _sigmoid(c)
    o_ref[...] = _dot(cond.astype(jnp.bfloat16), w_ref[...].astype(jnp.bfloat16)) + b_ref[...]


def _mod_call(c, mod_w, mod_b):
    depth, d, d3 = mod_w.shape
    b = c.shape[0]
    nb = d3 // d
    return pl.pallas_call(
        _mod_kernel,
        grid=(depth, nb),
        in_specs=[
            pl.BlockSpec((b, d), lambda l, n: (0, 0)),
            pl.BlockSpec((None, d, d), lambda l, n: (l, 0, n)),
            pl.BlockSpec((None, 1, d), lambda l, n: (l, 0, n)),
        ],
        out_specs=pl.BlockSpec((None, b, d), lambda l, n: (l, 0, n)),
        out_shape=jax.ShapeDtypeStruct((depth, b, d3), jnp.float32),
        compiler_params=pltpu.CompilerParams(
            dimension_semantics=("arbitrary", "arbitrary"),
            vmem_limit_bytes=VMEM_LIMIT_BYTES),
        name="adaln_mod",
    )(c, mod_w, mod_b.reshape(depth, 1, d3))


def _layer_a_kernel(xc_ref, xp_ref, modc_ref, modp_ref, pre_ref, post_ref, win_hbm, vnorm_ref,
                    ws_ref, bs_ref, wout_hbm, o_ref,
                    win_scr, wout_scr, wsem, h_scr, v_scr, y_scr, wm_scr,
                    *, tm, d, e, nsteps, widx):
    gw = e // GROUPS
    n_out = d // MXU_N
    i = pl.program_id(0)

    @pl.when(i == 0)
    def _():
        copies = (_weight_copies(win_hbm, widx, win_scr, wsem, 0)
                  + _weight_copies(wout_hbm, widx, wout_scr, wsem, win_scr.shape[0]))
        for cp in copies:
            cp.start()
        row = jax.lax.broadcasted_iota(jnp.int32, (CHUNK, CHUNK), 0)
        col = jax.lax.broadcasted_iota(jnp.int32, (CHUNK, CHUNK), 1)
        for g in range(GROUPS):
            wm_scr[g] = jnp.where(col <= row, ws_ref[g], 0.0)
        y_scr[...] = jnp.zeros((tm, e), jnp.float32)
        for cp in copies:
            cp.wait()

    def out_proj():
        return jnp.concatenate([_dot(y_scr[...], wout_scr[n]) for n in range(n_out)], axis=1)

    def out_stage(y):
        gate = modp_ref[:, 2 * d:]
        ms2 = jnp.mean(y * y, axis=-1, keepdims=True)
        o_ref[...] = xp_ref[...] + y * jax.lax.rsqrt(ms2 + EPS) * (post_ref[...] * gate)

    @pl.when(i < nsteps)
    def _():
        y_prev = out_proj()

        x = xc_ref[...]
        shift, scale = modc_ref[:, :d], modc_ref[:, d:2 * d]
        ms = jnp.mean(x * x, axis=-1, keepdims=True)
        h_scr[...] = x * jax.lax.rsqrt(ms + EPS) * (pre_ref[...] * (1.0 + scale)) + shift

        v_raw = _dot(h_scr[...], win_scr[GROUPS])
        out_stage(y_prev)

        s1 = jnp.zeros((tm, 1), jnp.float32)
        s2 = jnp.zeros((tm, 1), jnp.float32)
        for g in range(GROUPS):
            if g + 1 < GROUPS:
                nxt = _dot(h_scr[...], win_scr[GROUPS + g + 1])
            else:
                u_raw = _dot(h_scr[...], win_scr[0])
                g_raw = _dot(h_scr[...], win_scr[2 * GROUPS])
            vc = _gelu(v_raw)
            v_scr[:, g * gw:(g + 1) * gw] = vc
            s1 = s1 + jnp.sum(vc, axis=-1, keepdims=True)
            s2 = s2 + jnp.sum(vc * vc, axis=-1, keepdims=True)
            if g + 1 < GROUPS:
                v_raw = nxt
        mean = s1 * (1.0 / e)
        var = s2 * (1.0 / e) - mean * mean
        rstd = jax.lax.rsqrt(var + EPS)

        for g in range(GROUPS):
            cols = slice(g * gw, (g + 1) * gw)
            vn = ((v_scr[:, cols] - mean) * rstd * vnorm_ref[:, cols]).astype(jnp.bfloat16)
            bias = bs_ref[g]
            bias = jnp.concatenate([bias] * (gw // 128), axis=1)
            parts = [_dot(wm_scr[g], vn[j * CHUNK:(j + 1) * CHUNK, :]) + bias
                     for j in range(tm // CHUNK)]
            mixed = jnp.concatenate(parts, axis=0)
            if g + 1 < GROUPS:
                u_nxt = _dot(h_scr[...], win_scr[g + 1])
                g_nxt = _dot(h_scr[...], win_scr[2 * GROUPS + g + 1])
            y_scr[:, cols] = _gelu(u_raw) * mixed * _silu(g_raw)
            if g + 1 < GROUPS:
                u_raw, g_raw = u_nxt, g_nxt

    @pl.when(i == nsteps)
    def _():
        out_stage(out_proj())


def _layer_a(x, mod, layer, widx, pre, post, w_in, v_norm, w_s, bs_lanes, w_out, *, tm):
    b, s, d = x.shape
    e = w_out.shape[1]
    depth = mod.shape[0]
    tpb = s // tm
    nsteps = b * tpb
    cur = lambda i: jnp.minimum(i, nsteps - 1)
    prev = lambda i: jnp.maximum(i - 1, 0)
    single = pl.Buffered(1)
    kern = functools.partial(_layer_a_kernel, tm=tm, d=d, e=e, nsteps=nsteps, widx=widx)
    out = pl.pallas_call(
        kern,
        grid=(nsteps + 1,),
        in_specs=[
            pl.BlockSpec((tm, d), lambda i: (cur(i), 0)),
            pl.BlockSpec((tm, d), lambda i: (prev(i), 0)),
            pl.BlockSpec((None, None, 1, 3 * d), lambda i: (layer, cur(i) // tpb, 0, 0)),
            pl.BlockSpec((None, None, 1, 3 * d), lambda i: (layer, prev(i) // tpb, 0, 0)),
            pl.BlockSpec((None, 1, d), lambda i: (layer, 0, 0), pipeline_mode=single),
            pl.BlockSpec((None, 1, d), lambda i: (layer, 0, 0), pipeline_mode=single),
            pl.BlockSpec(memory_space=pl.ANY),
            pl.BlockSpec((None, 1, e), lambda i: (widx, 0, 0), pipeline_mode=single),
            pl.BlockSpec((None, GROUPS, CHUNK, CHUNK), lambda i: (widx, 0, 0, 0),
                         pipeline_mode=single),
            pl.BlockSpec((None, GROUPS, CHUNK, 128), lambda i: (widx, 0, 0, 0),
                         pipeline_mode=single),
            pl.BlockSpec(memory_space=pl.ANY),
        ],
        out_specs=pl.BlockSpec((tm, d), lambda i: (prev(i), 0)),
        out_shape=jax.ShapeDtypeStruct((b * s, d), jnp.float32),
        scratch_shapes=[
            pltpu.VMEM((3 * e // MXU_N, d, MXU_N), jnp.bfloat16),
            pltpu.VMEM((d // MXU_N, e, MXU_N), jnp.bfloat16),
            pltpu.SemaphoreType.DMA((3 * e // MXU_N + d // MXU_N,)),
            pltpu.VMEM((tm, d), jnp.float32),
            pltpu.VMEM((tm, e), jnp.float32),
            pltpu.VMEM((tm, e), jnp.float32),
            pltpu.VMEM((GROUPS, CHUNK, CHUNK), jnp.float32),
        ],
        compiler_params=pltpu.CompilerParams(
            dimension_semantics=("arbitrary",),
            vmem_limit_bytes=VMEM_LIMIT_BYTES),
        name="gmlp_layer",
    )(x.reshape(b * s, d), x.reshape(b * s, d), mod.reshape(depth, b, 1, 3 * d),
      mod.reshape(depth, b, 1, 3 * d), pre, post, w_in, v_norm, w_s, bs_lanes, w_out)
    return out.reshape(b, s, d)


X_SLOTS = 3
O_SLOTS = 2


def _x_copy(x_hbm, xbuf, sem, step, slot, b, tt):
    return pltpu.make_async_copy(x_hbm.at[b, pl.ds(step * tt, tt), :], xbuf.at[slot, :, b, :],
                                 sem.at[slot, b])


def _o_copy(obuf, o_hbm, sem, step, slot, b, tt):
    return pltpu.make_async_copy(obuf.at[slot, :, b, :], o_hbm.at[b, pl.ds(step * tt, tt), :],
                                 sem.at[slot, b])


def _layer_b_kernel(x_hbm, mod_ref, pre_ref, post_ref, win_hbm, cw_ref, cb_ref, gaw_ref,
                    gab_ref, gxw_ref, gxb_ref, lam_ref, wout_hbm, o_hbm,
                    xbuf, obuf, sem_in, sem_out, win_scr, wout_scr, wsem,
                    h_scr, xb_scr, hs_scr, y_scr, state_scr,
                    *, tt, nb, d, e, nsteps, widx):
    hw = e // GROUPS
    tm = tt * nb
    pad = CONV_WIDTH - 1
    n_out = d // MXU_N
    i = pl.program_id(0)

    @pl.when(i == 0)
    def _():
        copies = (_weight_copies(win_hbm, widx, win_scr, wsem, 0)
                  + _weight_copies(wout_hbm, widx, wout_scr, wsem, win_scr.shape[0]))
        for cp in copies:
            cp.start()
        for b in range(nb):
            _x_copy(x_hbm, xbuf, sem_in, 0, 0, b, tt).start()
        xb_scr[0:pad] = jnp.zeros((pad, nb, e), jnp.float32)
        state_scr[...] = jnp.zeros((nb, e), jnp.float32)
        y_scr[...] = jnp.zeros((tm, e), jnp.float32)
        xbuf[X_SLOTS - 1] = jnp.zeros((tt, nb, d), jnp.float32)
        for cp in copies:
            cp.wait()

    @pl.when(i + 1 < nsteps)
    def _():
        for b in range(nb):
            _x_copy(x_hbm, xbuf, sem_in, i + 1, (i + 1) % X_SLOTS, b, tt).start()

    @pl.when(i < nsteps)
    def _():
        for b in range(nb):
            _x_copy(x_hbm, xbuf, sem_in, i, i % X_SLOTS, b, tt).wait()

    @pl.when(i >= O_SLOTS + 1)
    def _():
        for b in range(nb):
            _o_copy(obuf, o_hbm, sem_out, i - 1 - O_SLOTS, (i - 1) % O_SLOTS, b, tt).wait()

    def out_tile(n):
        return _dot(y_scr[...], wout_scr[n])

    def out_proj():
        return jnp.concatenate([out_tile(n) for n in range(n_out)], axis=1)

    def out_stage(y):
        gate = mod_ref[:, 2 * d:]
        xp = xbuf[(i + X_SLOTS - 1) % X_SLOTS]
        y = y.reshape(tt, nb, d)
        ms2 = jnp.mean(y * y, axis=-1, keepdims=True)
        obuf[(i + O_SLOTS - 1) % O_SLOTS] = (
            xp + y * jax.lax.rsqrt(ms2 + EPS) * (post_ref[...] * gate)[None])

    @pl.when(i < nsteps)
    def _():
        early_tiles, late_tiles = (0, 1), (2, 3)
        y_tiles = [out_tile(n) for n in early_tiles]

        x = xbuf[i % X_SLOTS]
        shift, scale = mod_ref[:, :d], mod_ref[:, d:2 * d]
        ms = jnp.mean(x * x, axis=-1, keepdims=True)
        h = x * jax.lax.rsqrt(ms + EPS) * (pre_ref[...] * (1.0 + scale))[None] + shift[None]
        h_scr[...] = h.reshape(tm, d)

        def in_proj(hd):
            hcols = slice(hd * hw, (hd + 1) * hw)
            xb_scr[pad:pad + tt, :, hcols] = _dot(h_scr[...], win_scr[hd]).reshape(tt, nb, hw)

        in_proj(0)
        in_proj(1)

        lam = lam_ref[...]
        hkl = (-0.5 * LRU_C * _LOG2E) * (
            jnp.maximum(-lam, 0.0) + jnp.log(1.0 + jnp.exp(-jnp.abs(lam))))
        hcw = 0.5 * cw_ref[...]
        hcb = 0.5 * cb_ref[...]
        hgab = 0.5 * gab_ref[...]
        hgxb = 0.5 * gxb_ref[...]

        def conv_gates(hd):
            hcols = slice(hd * hw, (hd + 1) * hw)
            hxc = hcb[:, hcols][None] + xb_scr[0:tt, :, hcols] * hcw[0:1, hcols][None]
            for k in range(1, CONV_WIDTH):
                hxc = hxc + xb_scr[k:k + tt, :, hcols] * hcw[k:k + 1, hcols][None]
            hxc = hxc.reshape(tm, hw)
            zr = _dot(hxc, gaw_ref[hd]) + hgab[:, hcols]
            zi = _dot(hxc, gxw_ref[hd]) + hgxb[:, hcols]
            return hxc, zr, zi

        def recurrence(hd, hxc, zr, zi):
            hcols = slice(hd * hw, (hd + 1) * hw)
            tr = jnp.tanh(zr)
            ti = jnp.tanh(zi)
            a = jnp.exp2(tr * hkl[:, hcols] + hkl[:, hcols])
            q = 1.0 - a * a
            mult = q * jax.lax.rsqrt(jnp.maximum(q, 1e-30))
            bt = mult * (ti * hxc + hxc)
            a3 = a.reshape(tt, nb, hw)
            b3 = bt.reshape(tt, nb, hw)
            st = state_scr[:, hcols]
            for t in range(tt):
                st = a3[t] * st + b3[t]
                hs_scr[t, :, hcols] = st
            state_scr[:, hcols] = st

        def gate_out(hd, g_raw):
            hcols = slice(hd * hw, (hd + 1) * hw)
            y_scr[:, hcols] = hs_scr[:, :, hcols].reshape(tm, hw) * _silu(g_raw)

        G_LAG = 2
        g_raws = {}
        nxt = conv_gates(0)
        for hd in range(GROUPS):
            cur = nxt
            if hd + 2 < GROUPS:
                in_proj(hd + 2)
            if hd + 1 < GROUPS:
                nxt = conv_gates(hd + 1)
            if hd < len(late_tiles):
                y_tiles.append(out_tile(late_tiles[hd]))
                if hd == len(late_tiles) - 1:
                    out_stage(jnp.concatenate(y_tiles, axis=1))
            if hd - G_LAG >= 0:
                g_raws[hd - G_LAG] = _dot(h_scr[...], win_scr[GROUPS + hd - G_LAG])
            if hd == GROUPS - 1:
                for k in range(GROUPS - G_LAG, GROUPS):
                    g_raws[k] = _dot(h_scr[...], win_scr[GROUPS + k])
            recurrence(hd, *cur)
            if hd - G_LAG - 1 >= 0:
                gate_out(hd - G_LAG - 1, g_raws.pop(hd - G_LAG - 1))
        for hd in sorted(g_raws):
            gate_out(hd, g_raws[hd])

        xb_scr[0:pad] = xb_scr[tt:tt + pad]

    @pl.when(i == nsteps)
    def _():
        out_stage(out_proj())

    @pl.when(i >= 1)
    def _():
        for b in range(nb):
            _o_copy(obuf, o_hbm, sem_out, i - 1, (i - 1) % O_SLOTS, b, tt).start()

    @pl.when(i == nsteps)
    def _():
        for k in range(min(O_SLOTS, nsteps)):
            for b in range(nb):
                _o_copy(obuf, o_hbm, sem_out, i - 1 - k, (i - 1 - k) % O_SLOTS, b, tt).wait()


def _layer_b(x, mod, layer, widx, pre, post, w_in, conv_w, conv_b, ga_w, ga_b, gx_w, gx_b, lam,
             w_out, *, tt):
    nb, s, d = x.shape
    e = w_out.shape[1]
    hw = e // GROUPS
    tm = tt * nb
    nsteps = s // tt
    single = pl.Buffered(1)
    by_layer = lambda i: (layer, 0, 0)
    by_w = lambda i: (widx, 0, 0)
    by_w4 = lambda i: (widx, 0, 0, 0)
    kern = functools.partial(_layer_b_kernel, tt=tt, nb=nb, d=d, e=e, nsteps=nsteps, widx=widx)
    return pl.pallas_call(
        kern,
        grid=(nsteps + 1,),
        in_specs=[
            pl.BlockSpec(memory_space=pl.ANY),
            pl.BlockSpec((None, nb, 3 * d), by_layer, pipeline_mode=single),
            pl.BlockSpec((None, 1, d), by_layer, pipeline_mode=single),
            pl.BlockSpec((None, 1, d), by_layer, pipeline_mode=single),
            pl.BlockSpec(memory_space=pl.ANY),
            pl.BlockSpec((None, CONV_WIDTH, e), by_w, pipeline_mode=single),
            pl.BlockSpec((None, 1, e), by_w, pipeline_mode=single),
            pl.BlockSpec((None, GROUPS, hw, hw), by_w4, pipeline_mode=single),
            pl.BlockSpec((None, 1, e), by_w, pipeline_mode=single),
            pl.BlockSpec((None, GROUPS, hw, hw), by_w4, pipeline_mode=single),
            pl.BlockSpec((None, 1, e), by_w, pipeline_mode=single),
            pl.BlockSpec((None, 1, e), by_w, pipeline_mode=single),
            pl.BlockSpec(memory_space=pl.ANY),
        ],
        out_specs=pl.BlockSpec(memory_space=pl.ANY),
        out_shape=jax.ShapeDtypeStruct((nb, s, d), jnp.float32),
        scratch_shapes=[
            pltpu.VMEM((X_SLOTS, tt, nb, d), jnp.float32),
            pltpu.VMEM((O_SLOTS, tt, nb, d), jnp.float32),
            pltpu.SemaphoreType.DMA((X_SLOTS, nb)),
            pltpu.SemaphoreType.DMA((O_SLOTS, nb)),
            pltpu.VMEM((2 * e // MXU_N, d, MXU_N), jnp.bfloat16),
            pltpu.VMEM((d // MXU_N, e, MXU_N), jnp.bfloat16),
            pltpu.SemaphoreType.DMA((2 * e // MXU_N + d // MXU_N,)),
            pltpu.VMEM((tm, d), jnp.float32),
            pltpu.VMEM((tt + CONV_WIDTH - 1, nb, e), jnp.float32),
            pltpu.VMEM((tt, nb, e), jnp.float32),
            pltpu.VMEM((tm, e), jnp.float32),
            pltpu.VMEM((nb, e), jnp.float32),
        ],
        compiler_params=pltpu.CompilerParams(
            dimension_semantics=("arbitrary",),
            vmem_limit_bytes=VMEM_LIMIT_BYTES),
        name="rglru_layer",
    )(x, mod, pre, post, w_in, conv_w, conv_b, ga_w, ga_b, gx_w, gx_b, lam, w_out)


def kernel(x, c, mod_w, mod_b, pre_norm, post_norm, a_w_in, a_v_norm, a_w_s, a_b_s, a_w_out,
           b_w_in, b_conv_w, b_conv_b, b_ga_w, b_ga_b, b_gx_w, b_gx_b, b_lambda, b_w_out):
    depth, d = pre_norm.shape
    n_a, e = a_v_norm.shape
    n_b = b_lambda.shape[0]
    bf16 = jnp.bfloat16
    mod = _mod_call(c, mod_w, mod_b)
    pre = pre_norm.reshape(depth, 1, d)
    post = post_norm.reshape(depth, 1, d)
    a_w_in, a_w_out = a_w_in.astype(bf16), a_w_out.astype(bf16)
    b_w_in, b_w_out = b_w_in.astype(bf16), b_w_out.astype(bf16)
    b_ga_w, b_gx_w = b_ga_w.astype(bf16), b_gx_w.astype(bf16)
    a_v_norm = a_v_norm.reshape(n_a, 1, e)
    a_bs_lanes = jnp.broadcast_to(a_b_s[..., None], a_b_s.shape + (128,))
    b_conv_b, b_ga_b, b_gx_b, b_lambda = (
        p.reshape(n_b, 1, e) for p in (b_conv_b, b_ga_b, b_gx_b, b_lambda))
    for layer in range(depth):
        j = layer // 2
        if layer % 2 == 0:
            x = _layer_a(x, mod, layer, j, pre, post, a_w_in, a_v_norm, a_w_s, a_bs_lanes,
                         a_w_out, tm=512)
        else:
            x = _layer_b(x, mod, layer, j, pre, post, b_w_in, b_conv_w, b_conv_b, b_ga_w,
                         b_ga_b, b_gx_w, b_gx_b, b_lambda, b_w_out, tt=64)
    return x
```

```python
import functools
import math

import jax
import jax.numpy as jnp
from jax.experimental import pallas as pl
from jax.experimental.pallas import tpu as pltpu

EPS = 1e-6
CHUNK = 128
GROUPS = 8
CONV_WIDTH = 4
GATE_AHEAD = 2
LRU_C = 8.0
MXU_N = 256
VMEM_LIMIT_BYTES = 56 * 1024 * 1024

_GELU_C0 = math.sqrt(2.0 / math.pi)
_GELU_C1 = 0.044715
_LOG2E = 1.0 / math.log(2.0)


def _gelu(x):
    inner = x * (x * x * (_GELU_C0 * _GELU_C1) + _GELU_C0)
    hx = 0.5 * x
    return hx * jnp.tanh(inner) + hx


def _silu(x):
    hx = 0.5 * x
    return hx * jnp.tanh(hx) + hx


def _sigmoid(x):
    return 0.5 * jnp.tanh(0.5 * x) + 0.5


def _dot(a, b):
    return jnp.dot(a, b, preferred_element_type=jnp.float32)


def _weight_copies(w_hbm, widx, w_scr, sem, sem_base):
    n_tiles = w_scr.shape[0]
    return [pltpu.make_async_copy(w_hbm.at[widx, :, pl.ds(n * MXU_N, MXU_N)], w_scr.at[n],
                                  sem.at[sem_base + n]) for n in range(n_tiles)]


def _mod_kernel(c_ref, w_ref, b_ref, o_ref):
    c = c_ref[...]
    cond = c * _sigmoid(c)
    o_ref[...] = _dot(cond.astype(jnp.bfloat16), w_ref[...].astype(jnp.bfloat16)) + b_ref[...]


def _mod_call(c, mod_w, mod_b):
    depth, d, d3 = mod_w.shape
    b = c.shape[0]
    nb = d3 // d
    return pl.pallas_call(
        _mod_kernel,
        grid=(depth, nb),
        in_specs=[
            pl.BlockSpec((b, d), lambda l, n: (0, 0)),
            pl.BlockSpec((None, d, d), lambda l, n: (l, 0, n)),
            pl.BlockSpec((None, 1, d), lambda l, n: (l, 0, n)),
        ],
        out_specs=pl.BlockSpec((None, b, d), lambda l, n: (l, 0, n)),
        out_shape=jax.ShapeDtypeStruct((depth, b, d3), jnp.float32),
        compiler_params=pltpu.CompilerParams(
            dimension_semantics=("arbitrary", "arbitrary"),
            vmem_limit_bytes=VMEM_LIMIT_BYTES),
        name="adaln_mod",
    )(c, mod_w, mod_b.reshape(depth, 1, d3))


def _layer_a_kernel(xc_ref, xp_ref, modc_ref, modp_ref, pre_ref, post_ref, win_hbm, vnorm_ref,
                    ws_ref, bs_ref, wout_hbm, o_ref,
                    win_scr, wout_scr, wsem, h_scr, v_scr, y_scr, g_scr, yo_scr, wm_scr,
                    *, tm, d, e, nsteps, widx):
    gw = e // GROUPS
    n_out = d // MXU_N
    n_rc = tm // CHUNK
    half = tm // 2
    i = pl.program_id(0)

    def rows(j):
        return slice(j * CHUNK, (j + 1) * CHUNK)

    def gcols(g):
        return slice(g * gw, (g + 1) * gw)

    @pl.when(i == 0)
    def _():
        copies = (_weight_copies(win_hbm, widx, win_scr, wsem, 0)
                  + _weight_copies(wout_hbm, widx, wout_scr, wsem, win_scr.shape[0]))
        for cp in copies:
            cp.start()
        row = jax.lax.broadcasted_iota(jnp.int32, (CHUNK, CHUNK), 0)
        col = jax.lax.broadcasted_iota(jnp.int32, (CHUNK, CHUNK), 1)
        for g in range(GROUPS):
            wm_scr[g] = jnp.where(col <= row, ws_ref[g], 0.0)
        y_scr[...] = jnp.zeros((tm, e), jnp.float32)
        for cp in copies:
            cp.wait()

    def out_proj():
        for n in range(n_out):
            yo_scr[:, n * MXU_N:(n + 1) * MXU_N] = _dot(y_scr[...], wout_scr[n])

    def out_piece(j):
        y = yo_scr[rows(j), :]
        gate = modp_ref[:, 2 * d:]
        ms2 = jnp.mean(y * y, axis=-1, keepdims=True)
        o_ref[rows(j), :] = (
            xp_ref[rows(j), :] + y * jax.lax.rsqrt(ms2 + EPS) * (post_ref[...] * gate))

    @pl.when(i < nsteps)
    def _():
        out_proj()

        x = xc_ref[...]
        shift, scale = modc_ref[:, :d], modc_ref[:, d:2 * d]
        ms = jnp.mean(x * x, axis=-1, keepdims=True)
        h_scr[...] = x * jax.lax.rsqrt(ms + EPS) * (pre_ref[...] * (1.0 + scale)) + shift

        def proj(g):
            v_scr[:, gcols(g)] = _dot(h_scr[...], win_scr[GROUPS + g])
            y_scr[:, gcols(g)] = _dot(h_scr[...], win_scr[g])

        def gate_proj(g):
            g_scr[:, gcols(g)] = _dot(h_scr[...], win_scr[2 * GROUPS + g])

        s1 = [jnp.zeros((CHUNK, 1), jnp.float32) for _ in range(n_rc)]
        s2 = [jnp.zeros((CHUNK, 1), jnp.float32) for _ in range(n_rc)]
        proj(0)
        for g in range(GROUPS):
            if g + 1 < GROUPS:
                proj(g + 1)
            for j in range(n_rc):
                vc = _gelu(v_scr[rows(j), gcols(g)])
                v_scr[rows(j), gcols(g)] = vc
                s1[j] = s1[j] + jnp.sum(vc, axis=-1, keepdims=True)
                s2[j] = s2[j] + jnp.sum(vc * vc, axis=-1, keepdims=True)
                y_scr[rows(j), gcols(g)] = _gelu(y_scr[rows(j), gcols(g)])
            if g < n_rc:
                out_piece(g)
        for k in range(GATE_AHEAD):
            gate_proj(k)
        mean = [s * (1.0 / e) for s in s1]
        rstd = [jax.lax.rsqrt(q * (1.0 / e) - m * m + EPS) for q, m in zip(s2, mean)]

        def mix(g):
            for j in range(n_rc):
                vn = (v_scr[rows(j), gcols(g)] - mean[j]) * rstd[j] * vnorm_ref[:, gcols(g)]
                v_scr[rows(j), gcols(g)] = _dot(wm_scr[g], vn.astype(jnp.bfloat16))

        mix(0)
        for g in range(GROUPS):
            if g + GATE_AHEAD < GROUPS:
                gate_proj(g + GATE_AHEAD)
            if g + 1 < GROUPS:
                mix(g + 1)
            bias = jnp.concatenate([bs_ref[g]] * (half // CHUNK), axis=0)
            bias = jnp.concatenate([bias] * (gw // 128), axis=1)
            for k in range(2):
                hr = slice(k * half, (k + 1) * half)
                y_scr[hr, gcols(g)] = (y_scr[hr, gcols(g)] * (v_scr[hr, gcols(g)] + bias)
                                       * _silu(g_scr[hr, gcols(g)]))

    @pl.when(i == nsteps)
    def _():
        out_proj()
        for j in range(n_rc):
            out_piece(j)


def _layer_a(x, mod, layer, widx, pre, post, w_in, v_norm, w_s, bs_lanes, w_out, *, tm):
    b, s, d = x.shape
    e = w_out.shape[1]
    depth = mod.shape[0]
    tpb = s // tm
    nsteps = b * tpb
    cur = lambda i: jnp.minimum(i, nsteps - 1)
    prev = lambda i: jnp.maximum(i - 1, 0)
    single = pl.Buffered(1)
    kern = functools.partial(_layer_a_kernel, tm=tm, d=d, e=e, nsteps=nsteps, widx=widx)
    out = pl.pallas_call(
        kern,
        grid=(nsteps + 1,),
        in_specs=[
            pl.BlockSpec((tm, d), lambda i: (cur(i), 0)),
            pl.BlockSpec((tm, d), lambda i: (prev(i), 0)),
            pl.BlockSpec((None, None, 1, 3 * d), lambda i: (layer, cur(i) // tpb, 0, 0)),
            pl.BlockSpec((None, None, 1, 3 * d), lambda i: (layer, prev(i) // tpb, 0, 0)),
            pl.BlockSpec((None, 1, d), lambda i: (layer, 0, 0), pipeline_mode=single),
            pl.BlockSpec((None, 1, d), lambda i: (layer, 0, 0), pipeline_mode=single),
            pl.BlockSpec(memory_space=pl.ANY),
            pl.BlockSpec((None, 1, e), lambda i: (widx, 0, 0), pipeline_mode=single),
            pl.BlockSpec((None, GROUPS, CHUNK, CHUNK), lambda i: (widx, 0, 0, 0),
                         pipeline_mode=single),
            pl.BlockSpec((None, GROUPS, CHUNK, 128), lambda i: (widx, 0, 0, 0),
                         pipeline_mode=single),
            pl.BlockSpec(memory_space=pl.ANY),
        ],
        out_specs=pl.BlockSpec((tm, d), lambda i: (prev(i), 0)),
        out_shape=jax.ShapeDtypeStruct((b * s, d), jnp.float32),
        scratch_shapes=[
            pltpu.VMEM((3 * e // MXU_N, d, MXU_N), jnp.bfloat16),
            pltpu.VMEM((d // MXU_N, e, MXU_N), jnp.bfloat16),
            pltpu.SemaphoreType.DMA((3 * e // MXU_N + d // MXU_N,)),
            pltpu.VMEM((tm, d), jnp.float32),
            pltpu.VMEM((tm, e), jnp.float32),
            pltpu.VMEM((tm, e), jnp.float32),
            pltpu.VMEM((tm, e), jnp.float32),
            pltpu.VMEM((tm, d), jnp.float32),
            pltpu.VMEM((GROUPS, CHUNK, CHUNK), jnp.float32),
        ],
        compiler_params=pltpu.CompilerParams(
            dimension_semantics=("arbitrary",),
            vmem_limit_bytes=VMEM_LIMIT_BYTES),
        name="gmlp_layer",
    )(x.reshape(b * s, d), x.reshape(b * s, d), mod.reshape(depth, b, 1, 3 * d),
      mod.reshape(depth, b, 1, 3 * d), pre, post, w_in, v_norm, w_s, bs_lanes, w_out)
    return out.reshape(b, s, d)


X_SLOTS = 3
O_SLOTS = 2


def _x_copy(x_hbm, xbuf, sem, step, slot, b, tt):
    return pltpu.make_async_copy(x_hbm.at[b, pl.ds(step * tt, tt), :], xbuf.at[slot, :, b, :],
                                 sem.at[slot, b])


def _o_copy(obuf, o_hbm, sem, step, slot, b, tt):
    return pltpu.make_async_copy(obuf.at[slot, :, b, :], o_hbm.at[b, pl.ds(step * tt, tt), :],
                                 sem.at[slot, b])


def _layer_b_kernel(x_hbm, mod_ref, pre_ref, post_ref, win_hbm, cw_ref, cb_ref, gaw_ref,
                    gab_ref, gxw_ref, gxb_ref, lam_ref, wout_hbm, o_hbm,
                    xbuf, obuf, sem_in, sem_out, win_scr, wout_scr, wsem,
                    h_scr, xb_scr, hs_scr, y_scr, state_scr,
                    *, tt, nb, d, e, nsteps, widx):
    hw = e // GROUPS
    tm = tt * nb
    pad = CONV_WIDTH - 1
    n_out = d // MXU_N
    i = pl.program_id(0)

    @pl.when(i == 0)
    def _():
        copies = (_weight_copies(win_hbm, widx, win_scr, wsem, 0)
                  + _weight_copies(wout_hbm, widx, wout_scr, wsem, win_scr.shape[0]))
        for cp in copies:
            cp.start()
        for b in range(nb):
            _x_copy(x_hbm, xbuf, sem_in, 0, 0, b, tt).start()
        xb_scr[0:pad] = jnp.zeros((pad, nb, e), jnp.float32)
        state_scr[...] = jnp.zeros((nb, e), jnp.float32)
        y_scr[...] = jnp.zeros((tm, e), jnp.float32)
        xbuf[X_SLOTS - 1] = jnp.zeros((tt, nb, d), jnp.float32)
        for cp in copies:
            cp.wait()

    @pl.when(i + 1 < nsteps)
    def _():
        for b in range(nb):
            _x_copy(x_hbm, xbuf, sem_in, i + 1, (i + 1) % X_SLOTS, b, tt).start()

    @pl.when(i < nsteps)
    def _():
        for b in range(nb):
            _x_copy(x_hbm, xbuf, sem_in, i, i % X_SLOTS, b, tt).wait()

    @pl.when(i >= O_SLOTS + 1)
    def _():
        for b in range(nb):
            _o_copy(obuf, o_hbm, sem_out, i - 1 - O_SLOTS, (i - 1) % O_SLOTS, b, tt).wait()

    def out_tile(n):
        return _dot(y_scr[...], wout_scr[n])

    def out_proj():
        return jnp.concatenate([out_tile(n) for n in range(n_out)], axis=1)

    def out_stage(y):
        gate = mod_ref[:, 2 * d:]
        xp = xbuf[(i + X_SLOTS - 1) % X_SLOTS]
        y = y.reshape(tt, nb, d)
        ms2 = jnp.mean(y * y, axis=-1, keepdims=True)
        obuf[(i + O_SLOTS - 1) % O_SLOTS] = (
            xp + y * jax.lax.rsqrt(ms2 + EPS) * (post_ref[...] * gate)[None])

    @pl.when(i < nsteps)
    def _():
        early_tiles, late_tiles = (0, 1), (2, 3)
        y_tiles = [out_tile(n) for n in early_tiles]

        x = xbuf[i % X_SLOTS]
        shift, scale = mod_ref[:, :d], mod_ref[:, d:2 * d]
        ms = jnp.mean(x * x, axis=-1, keepdims=True)
        h = x * jax.lax.rsqrt(ms + EPS) * (pre_ref[...] * (1.0 + scale))[None] + shift[None]
        h_scr[...] = h.reshape(tm, d)

        def in_proj(hd):
            hcols = slice(hd * hw, (hd + 1) * hw)
            xb_scr[pad:pad + tt, :, hcols] = _dot(h_scr[...], win_scr[hd]).reshape(tt, nb, hw)

        in_proj(0)
        in_proj(1)

        lam = lam_ref[...]
        hkl = (-0.5 * LRU_C * _LOG2E) * (
            jnp.maximum(-lam, 0.0) + jnp.log(1.0 + jnp.exp(-jnp.abs(lam))))
        hcw = 0.5 * cw_ref[...]
        hcb = 0.5 * cb_ref[...]
        hgab = 0.5 * gab_ref[...]
        hgxb = 0.5 * gxb_ref[...]

        def conv_gates(hd):
            hcols = slice(hd * hw, (hd + 1) * hw)
            hxc = hcb[:, hcols][None] + xb_scr[0:tt, :, hcols] * hcw[0:1, hcols][None]
            for k in range(1, CONV_WIDTH):
                hxc = hxc + xb_scr[k:k + tt, :, hcols] * hcw[k:k + 1, hcols][None]
            hxc = hxc.reshape(tm, hw)
            zr = _dot(hxc, gaw_ref[hd]) + hgab[:, hcols]
            zi = _dot(hxc, gxw_ref[hd]) + hgxb[:, hcols]
            return hxc, zr, zi

        def recurrence(hd, hxc, zr, zi):
            hcols = slice(hd * hw, (hd + 1) * hw)
            tr = jnp.tanh(zr)
            ti = jnp.tanh(zi)
            a = jnp.exp2(tr * hkl[:, hcols] + hkl[:, hcols])
            q = 1.0 - a * a
            mult = q * jax.lax.rsqrt(jnp.maximum(q, 1e-30))
            bt = mult * (ti * hxc + hxc)
            a3 = a.reshape(tt, nb, hw)
            b3 = bt.reshape(tt, nb, hw)
            st = state_scr[:, hcols]
            for t in range(tt):
                st = a3[t] * st + b3[t]
                hs_scr[t, :, hcols] = st
            state_scr[:, hcols] = st

        def gate_out(hd, g_raw):
            hcols = slice(hd * hw, (hd + 1) * hw)
            y_scr[:, hcols] = hs_scr[:, :, hcols].reshape(tm, hw) * _silu(g_raw)

        G_LAG = 2
        g_raws = {}
        nxt = conv_gates(0)
        for hd in range(GROUPS):
            cur = nxt
            if hd + 2 < GROUPS:
                in_proj(hd + 2)
            if hd + 1 < GROUPS:
                nxt = conv_gates(hd + 1)
            if hd < len(late_tiles):
                y_tiles.append(out_tile(late_tiles[hd]))
                if hd == len(late_tiles) - 1:
                    out_stage(jnp.concatenate(y_tiles, axis=1))
            if hd - G_LAG >= 0:
                g_raws[hd - G_LAG] = _dot(h_scr[...], win_scr[GROUPS + hd - G_LAG])
            if hd == GROUPS - 1:
                for k in range(GROUPS - G_LAG, GROUPS):
                    g_raws[k] = _dot(h_scr[...], win_scr[GROUPS + k])
            recurrence(hd, *cur)
            if hd - G_LAG - 1 >= 0:
                gate_out(hd - G_LAG - 1, g_raws.pop(hd - G_LAG - 1))
        for hd in sorted(g_raws):
            gate_out(hd, g_raws[hd])

        xb_scr[0:pad] = xb_scr[tt:tt + pad]

    @pl.when(i == nsteps)
    def _():
        out_stage(out_proj())

    @pl.when(i >= 1)
    def _():
        for b in range(nb):
            _o_copy(obuf, o_hbm, sem_out, i - 1, (i - 1) % O_SLOTS, b, tt).start()

    @pl.when(i == nsteps)
    def _():
        for k in range(min(O_SLOTS, nsteps)):
            for b in range(nb):
                _o_copy(obuf, o_hbm, sem_out, i - 1 - k, (i - 1 - k) % O_SLOTS, b, tt).wait()


def _layer_b(x, mod, layer, widx, pre, post, w_in, conv_w, conv_b, ga_w, ga_b, gx_w, gx_b, lam,
             w_out, *, tt):
    nb, s, d = x.shape
    e = w_out.shape[1]
    hw = e // GROUPS
    tm = tt * nb
    nsteps = s // tt
    single = pl.Buffered(1)
    by_layer = lambda i: (layer, 0, 0)
    by_w = lambda i: (widx, 0, 0)
    by_w4 = lambda i: (widx, 0, 0, 0)
    kern = functools.partial(_layer_b_kernel, tt=tt, nb=nb, d=d, e=e, nsteps=nsteps, widx=widx)
    return pl.pallas_call(
        kern,
        grid=(nsteps + 1,),
        in_specs=[
            pl.BlockSpec(memory_space=pl.ANY),
            pl.BlockSpec((None, nb, 3 * d), by_layer, pipeline_mode=single),
            pl.BlockSpec((None, 1, d), by_layer, pipeline_mode=single),
            pl.BlockSpec((None, 1, d), by_layer, pipeline_mode=single),
            pl.BlockSpec(memory_space=pl.ANY),
            pl.BlockSpec((None, CONV_WIDTH, e), by_w, pipeline_mode=single),
            pl.BlockSpec((None, 1, e), by_w, pipeline_mode=single),
            pl.BlockSpec((None, GROUPS, hw, hw), by_w4, pipeline_mode=single),
            pl.BlockSpec((None, 1, e), by_w, pipeline_mode=single),
            pl.BlockSpec((None, GROUPS, hw, hw), by_w4, pipeline_mode=single),
            pl.BlockSpec((None, 1, e), by_w, pipeline_mode=single),
            pl.BlockSpec((None, 1, e), by_w, pipeline_mode=single),
            pl.BlockSpec(memory_space=pl.ANY),
        ],
        out_specs=pl.BlockSpec(memory_space=pl.ANY),
        out_shape=jax.ShapeDtypeStruct((nb, s, d), jnp.float32),
        scratch_shapes=[
            pltpu.VMEM((X_SLOTS, tt, nb, d), jnp.float32),
            pltpu.VMEM((O_SLOTS, tt, nb, d), jnp.float32),
            pltpu.SemaphoreType.DMA((X_SLOTS, nb)),
            pltpu.SemaphoreType.DMA((O_SLOTS, nb)),
            pltpu.VMEM((2 * e // MXU_N, d, MXU_N), jnp.bfloat16),
            pltpu.VMEM((d // MXU_N, e, MXU_N), jnp.bfloat16),
            pltpu.SemaphoreType.DMA((2 * e // MXU_N + d // MXU_N,)),
            pltpu.VMEM((tm, d), jnp.float32),
            pltpu.VMEM((tt + CONV_WIDTH - 1, nb, e), jnp.float32),
            pltpu.VMEM((tt, nb, e), jnp.float32),
            pltpu.VMEM((tm, e), jnp.float32),
            pltpu.VMEM((nb, e), jnp.float32),
        ],
        compiler_params=pltpu.CompilerParams(
            dimension_semantics=("arbitrary",),
            vmem_limit_bytes=VMEM_LIMIT_BYTES),
        name="rglru_layer",
    )(x, mod, pre, post, w_in, conv_w, conv_b, ga_w, ga_b, gx_w, gx_b, lam, w_out)


def kernel(x, c, mod_w, mod_b, pre_norm, post_norm, a_w_in, a_v_norm, a_w_s, a_b_s, a_w_out,
           b_w_in, b_conv_w, b_conv_b, b_ga_w, b_ga_b, b_gx_w, b_gx_b, b_lambda, b_w_out):
    depth, d = pre_norm.shape
    n_a, e = a_v_norm.shape
    n_b = b_lambda.shape[0]
    bf16 = jnp.bfloat16
    mod = _mod_call(c, mod_w, mod_b)
    pre = pre_norm.reshape(depth, 1, d)
    post = post_norm.reshape(depth, 1, d)
    a_w_in, a_w_out = a_w_in.astype(bf16), a_w_out.astype(bf16)
    b_w_in, b_w_out = b_w_in.astype(bf16), b_w_out.astype(bf16)
    b_ga_w, b_gx_w = b_ga_w.astype(bf16), b_gx_w.astype(bf16)
    a_v_norm = a_v_norm.reshape(n_a, 1, e)
    a_bs_lanes = jnp.broadcast_to(a_b_s[..., None], a_b_s.shape + (128,))
    b_conv_b, b_ga_b, b_gx_b, b_lambda = (
        p.reshape(n_b, 1, e) for p in (b_conv_b, b_ga_b, b_gx_b, b_lambda))
    for layer in range(depth):
        j = layer // 2
        if layer % 2 == 0:
            x = _layer_a(x, mod, layer, j, pre, post, a_w_in, a_v_norm, a_w_s, a_bs_lanes,
                         a_w_out, tm=512)
        else:
            x = _layer_b(x, mod, layer, j, pre, post, b_w_in, b_conv_w, b_conv_b, b_ga_w,
                         b_ga_b, b_gx_w, b_gx_b, b_lambda, b_w_out, tt=64)
    return x
```

```python
import functools
import math

import jax
import jax.numpy as jnp
from jax.experimental import pallas as pl
from jax.experimental.pallas import tpu as pltpu

EPS = 1e-6
CHUNK = 128
GROUPS = 8
CONV_WIDTH = 4
GATE_AHEAD = 2
LRU_C = 8.0
MXU_N = 256
VMEM_LIMIT_BYTES = 56 * 1024 * 1024

_GELU_C0 = math.sqrt(2.0 / math.pi)
_GELU_C1 = 0.044715
_LOG2E = 1.0 / math.log(2.0)


def _gelu(x):
    inner = x * (x * x * (_GELU_C0 * _GELU_C1) + _GELU_C0)
    hx = 0.5 * x
    return hx * jnp.tanh(inner) + hx


def _silu(x):
    hx = 0.5 * x
    return hx * jnp.tanh(hx) + hx


def _sigmoid(x):
    return 0.5 * jnp.tanh(0.5 * x) + 0.5


def _dot(a, b):
    return jnp.dot(a, b, preferred_element_type=jnp.float32)


def _weight_copies(w_hbm, widx, w_scr, sem, sem_base):
    n_tiles = w_scr.shape[0]
    return [pltpu.make_async_copy(w_hbm.at[widx, :, pl.ds(n * MXU_N, MXU_N)], w_scr.at[n],
                                  sem.at[sem_base + n]) for n in range(n_tiles)]


def _mod_kernel(c_ref, w_ref, b_ref, o_ref):
    c = c_ref[...]
    cond = c * _sigmoid(c)
    o_ref[...] = _dot(cond.astype(jnp.bfloat16), w_ref[...].astype(jnp.bfloat16)) + b_ref[...]


def _mod_call(c, mod_w, mod_b):
    depth, d, d3 = mod_w.shape
    b = c.shape[0]
    nb = d3 // d
    return pl.pallas_call(
        _mod_kernel,
        grid=(depth, nb),
        in_specs=[
            pl.BlockSpec((b, d), lambda l, n: (0, 0)),
            pl.BlockSpec((None, d, d), lambda l, n: (l, 0, n)),
            pl.BlockSpec((None, 1, d), lambda l, n: (l, 0, n)),
        ],
        out_specs=pl.BlockSpec((None, b, d), lambda l, n: (l, 0, n)),
        out_shape=jax.ShapeDtypeStruct((depth, b, d3), jnp.float32),
        compiler_params=pltpu.CompilerParams(
            dimension_semantics=("arbitrary", "arbitrary"),
            vmem_limit_bytes=VMEM_LIMIT_BYTES),
        name="adaln_mod",
    )(c, mod_w, mod_b.reshape(depth, 1, d3))


def _layer_a_kernel(xc_ref, xp_ref, mod_ref, pre_ref, post_ref, win_hbm, vnorm_ref,
                    ws_ref, bs_ref, wout_hbm, o_ref,
                    win_scr, wout_scr, wsem, h_scr, v_scr, y_scr, g_scr, yo_scr, wm_scr,
                    *, tm, d, e, nsteps, tpb, layer, widx):
    gw = e // GROUPS
    n_out = d // MXU_N
    n_rc = tm // CHUNK
    half = tm // 2
    i = pl.program_id(0)
    b_cur = jnp.minimum(i, nsteps - 1) // tpb
    b_prev = jnp.maximum(i - 1, 0) // tpb

    def rows(j):
        return slice(j * CHUNK, (j + 1) * CHUNK)

    def gcols(g):
        return slice(g * gw, (g + 1) * gw)

    @pl.when(i == 0)
    def _():
        copies = (_weight_copies(win_hbm, widx, win_scr, wsem, 0)
                  + _weight_copies(wout_hbm, widx, wout_scr, wsem, win_scr.shape[0]))
        for cp in copies:
            cp.start()
        row = jax.lax.broadcasted_iota(jnp.int32, (CHUNK, CHUNK), 0)
        col = jax.lax.broadcasted_iota(jnp.int32, (CHUNK, CHUNK), 1)
        for g in range(GROUPS):
            wm_scr[g] = jnp.where(col <= row, ws_ref[g], 0.0)
        y_scr[...] = jnp.zeros((tm, e), jnp.float32)
        for cp in copies:
            cp.wait()

    def out_proj():
        for n in range(n_out):
            yo_scr[:, n * MXU_N:(n + 1) * MXU_N] = _dot(y_scr[...], wout_scr[n])

    def out_piece(j):
        y = yo_scr[rows(j), :]
        gate = mod_ref[pl.ds(b_prev, 1), 2 * d:]
        ms2 = jnp.mean(y * y, axis=-1, keepdims=True)
        o_ref[rows(j), :] = (xp_ref[rows(j), :] + y * jax.lax.rsqrt(ms2 + EPS)
                             * (post_ref[layer:layer + 1, :] * gate))

    @pl.when(i < nsteps)
    def _():
        out_proj()

        x = xc_ref[...]
        shift, scale = mod_ref[pl.ds(b_cur, 1), :d], mod_ref[pl.ds(b_cur, 1), d:2 * d]
        ms = jnp.mean(x * x, axis=-1, keepdims=True)
        h_scr[...] = (x * jax.lax.rsqrt(ms + EPS) * (pre_ref[layer:layer + 1, :] * (1.0 + scale))
                      + shift)

        def proj(g):
            v_scr[:, gcols(g)] = _dot(h_scr[...], win_scr[GROUPS + g])
            y_scr[:, gcols(g)] = _dot(h_scr[...], win_scr[g])

        def gate_proj(g):
            g_scr[:, gcols(g)] = _dot(h_scr[...], win_scr[2 * GROUPS + g])

        s1 = [jnp.zeros((CHUNK, 1), jnp.float32) for _ in range(n_rc)]
        s2 = [jnp.zeros((CHUNK, 1), jnp.float32) for _ in range(n_rc)]
        proj(0)
        for g in range(GROUPS):
            if g + 1 < GROUPS:
                proj(g + 1)
            for j in range(n_rc):
                vc = _gelu(v_scr[rows(j), gcols(g)])
                v_scr[rows(j), gcols(g)] = vc
                s1[j] = s1[j] + jnp.sum(vc, axis=-1, keepdims=True)
                s2[j] = s2[j] + jnp.sum(vc * vc, axis=-1, keepdims=True)
                y_scr[rows(j), gcols(g)] = _gelu(y_scr[rows(j), gcols(g)])
            if g < n_rc:
                out_piece(g)
        for k in range(GATE_AHEAD):
            gate_proj(k)
        mean = [s * (1.0 / e) for s in s1]
        rstd = [jax.lax.rsqrt(q * (1.0 / e) - m * m + EPS) for q, m in zip(s2, mean)]

        def mix(g):
            for j in range(n_rc):
                vn = (v_scr[rows(j), gcols(g)] - mean[j]) * rstd[j] * vnorm_ref[widx:widx + 1, gcols(g)]
                v_scr[rows(j), gcols(g)] = _dot(wm_scr[g], vn.astype(jnp.bfloat16))

        mix(0)
        for g in range(GROUPS):
            if g + GATE_AHEAD < GROUPS:
                gate_proj(g + GATE_AHEAD)
            if g + 1 < GROUPS:
                mix(g + 1)
            bias = jnp.concatenate([bs_ref[g]] * (half // CHUNK), axis=0)
            bias = jnp.concatenate([bias] * (gw // 128), axis=1)
            for k in range(2):
                hr = slice(k * half, (k + 1) * half)
                y_scr[hr, gcols(g)] = (y_scr[hr, gcols(g)] * (v_scr[hr, gcols(g)] + bias)
                                       * _silu(g_scr[hr, gcols(g)]))

    @pl.when(i == nsteps)
    def _():
        out_proj()
        for j in range(n_rc):
            out_piece(j)


def _layer_a(x, mod, layer, widx, pre, post, w_in, v_norm, w_s, bs_lanes, w_out, *, tm):
    b, s, d = x.shape
    e = w_out.shape[1]
    tpb = s // tm
    nsteps = b * tpb
    cur = lambda i: jnp.minimum(i, nsteps - 1)
    prev = lambda i: jnp.maximum(i - 1, 0)
    single = pl.Buffered(1)
    whole2 = lambda i: (0, 0)
    kern = functools.partial(_layer_a_kernel, tm=tm, d=d, e=e, nsteps=nsteps, tpb=tpb,
                             layer=layer, widx=widx)
    out = pl.pallas_call(
        kern,
        grid=(nsteps + 1,),
        in_specs=[
            pl.BlockSpec((tm, d), lambda i: (cur(i), 0)),
            pl.BlockSpec((tm, d), lambda i: (prev(i), 0)),
            pl.BlockSpec((None, b, 3 * d), lambda i: (layer, 0, 0), pipeline_mode=single),
            pl.BlockSpec(pre.shape, whole2, pipeline_mode=single),
            pl.BlockSpec(post.shape, whole2, pipeline_mode=single),
            pl.BlockSpec(memory_space=pl.ANY),
            pl.BlockSpec(v_norm.shape, whole2, pipeline_mode=single),
            pl.BlockSpec((None, GROUPS, CHUNK, CHUNK), lambda i: (widx, 0, 0, 0),
                         pipeline_mode=single),
            pl.BlockSpec((None, GROUPS, CHUNK, 128), lambda i: (widx, 0, 0, 0),
                         pipeline_mode=single),
            pl.BlockSpec(memory_space=pl.ANY),
        ],
        out_specs=pl.BlockSpec((tm, d), lambda i: (prev(i), 0)),
        out_shape=jax.ShapeDtypeStruct((b * s, d), jnp.float32),
        scratch_shapes=[
            pltpu.VMEM((3 * e // MXU_N, d, MXU_N), jnp.bfloat16),
            pltpu.VMEM((d // MXU_N, e, MXU_N), jnp.bfloat16),
            pltpu.SemaphoreType.DMA((3 * e // MXU_N + d // MXU_N,)),
            pltpu.VMEM((tm, d), jnp.float32),
            pltpu.VMEM((tm, e), jnp.float32),
            pltpu.VMEM((tm, e), jnp.float32),
            pltpu.VMEM((tm, e), jnp.float32),
            pltpu.VMEM((tm, d), jnp.float32),
            pltpu.VMEM((GROUPS, CHUNK, CHUNK), jnp.float32),
        ],
        compiler_params=pltpu.CompilerParams(
            dimension_semantics=("arbitrary",),
            vmem_limit_bytes=VMEM_LIMIT_BYTES),
        name="gmlp_layer",
    )(x.reshape(b * s, d), x.reshape(b * s, d), mod, pre, post, w_in, v_norm, w_s, bs_lanes,
      w_out)
    return out.reshape(b, s, d)


X_SLOTS = 3
O_SLOTS = 2


def _x_copy(x_hbm, xbuf, sem, step, slot, b, tt):
    return pltpu.make_async_copy(x_hbm.at[b, pl.ds(step * tt, tt), :], xbuf.at[slot, :, b, :],
                                 sem.at[slot, b])


def _o_copy(obuf, o_hbm, sem, step, slot, b, tt):
    return pltpu.make_async_copy(obuf.at[slot, :, b, :], o_hbm.at[b, pl.ds(step * tt, tt), :],
                                 sem.at[slot, b])


def _layer_b_kernel(x_hbm, mod_ref, pre_ref, post_ref, win_hbm, cw_ref, cb_ref, gaw_ref,
                    gab_ref, gxw_ref, gxb_ref, lam_ref, wout_hbm, o_hbm,
                    xbuf, obuf, sem_in, sem_out, win_scr, wout_scr, wsem,
                    h_scr, xb_scr, hs_scr, y_scr, state_scr,
                    *, tt, nb, d, e, nsteps, layer, widx):
    hw = e // GROUPS
    tm = tt * nb
    pad = CONV_WIDTH - 1
    n_out = d // MXU_N
    i = pl.program_id(0)

    @pl.when(i == 0)
    def _():
        copies = (_weight_copies(win_hbm, widx, win_scr, wsem, 0)
                  + _weight_copies(wout_hbm, widx, wout_scr, wsem, win_scr.shape[0]))
        for cp in copies:
            cp.start()
        for b in range(nb):
            _x_copy(x_hbm, xbuf, sem_in, 0, 0, b, tt).start()
        xb_scr[0:pad] = jnp.zeros((pad, nb, e), jnp.float32)
        state_scr[...] = jnp.zeros((nb, e), jnp.float32)
        y_scr[...] = jnp.zeros((tm, e), jnp.float32)
        xbuf[X_SLOTS - 1] = jnp.zeros((tt, nb, d), jnp.float32)
        for cp in copies:
            cp.wait()

    @pl.when(i + 1 < nsteps)
    def _():
        for b in range(nb):
            _x_copy(x_hbm, xbuf, sem_in, i + 1, (i + 1) % X_SLOTS, b, tt).start()

    @pl.when(i < nsteps)
    def _():
        for b in range(nb):
            _x_copy(x_hbm, xbuf, sem_in, i, i % X_SLOTS, b, tt).wait()

    @pl.when(i >= O_SLOTS + 1)
    def _():
        for b in range(nb):
            _o_copy(obuf, o_hbm, sem_out, i - 1 - O_SLOTS, (i - 1) % O_SLOTS, b, tt).wait()

    def out_tile(n):
        return _dot(y_scr[...], wout_scr[n])

    def out_proj():
        return jnp.concatenate([out_tile(n) for n in range(n_out)], axis=1)

    def out_stage(y):
        gate = mod_ref[:, 2 * d:]
        xp = xbuf[(i + X_SLOTS - 1) % X_SLOTS]
        y = y.reshape(tt, nb, d)
        ms2 = jnp.mean(y * y, axis=-1, keepdims=True)
        obuf[(i + O_SLOTS - 1) % O_SLOTS] = (
            xp + y * jax.lax.rsqrt(ms2 + EPS) * (post_ref[layer:layer + 1, :] * gate)[None])

    @pl.when(i < nsteps)
    def _():
        early_tiles, late_tiles = (0,), (1, 2, 3)
        y_tiles = [out_tile(n) for n in early_tiles]

        x = xbuf[i % X_SLOTS]
        shift, scale = mod_ref[:, :d], mod_ref[:, d:2 * d]
        ms = jnp.mean(x * x, axis=-1, keepdims=True)
        h = (x * jax.lax.rsqrt(ms + EPS) * (pre_ref[layer:layer + 1, :] * (1.0 + scale))[None]
             + shift[None])
        h_scr[...] = h.reshape(tm, d)

        def in_proj(hd):
            hcols = slice(hd * hw, (hd + 1) * hw)
            xb_scr[pad:pad + tt, :, hcols] = _dot(h_scr[...], win_scr[hd]).reshape(tt, nb, hw)

        in_proj(0)
        in_proj(1)

        lam = lam_ref[widx:widx + 1, :]
        hkl = (-0.5 * LRU_C * _LOG2E) * (
            jnp.maximum(-lam, 0.0) + jnp.log(1.0 + jnp.exp(-jnp.abs(lam))))
        hcw = 0.5 * cw_ref[...]
        hcb = 0.5 * cb_ref[widx:widx + 1, :]
        hgab = 0.5 * gab_ref[widx:widx + 1, :]
        hgxb = 0.5 * gxb_ref[widx:widx + 1, :]

        def conv_gates(hd):
            hcols = slice(hd * hw, (hd + 1) * hw)
            hxc = hcb[:, hcols][None] + xb_scr[0:tt, :, hcols] * hcw[0:1, hcols][None]
            for k in range(1, CONV_WIDTH):
                hxc = hxc + xb_scr[k:k + tt, :, hcols] * hcw[k:k + 1, hcols][None]
            hxc = hxc.reshape(tm, hw)
            zr = _dot(hxc, gaw_ref[hd]) + hgab[:, hcols]
            zi = _dot(hxc, gxw_ref[hd]) + hgxb[:, hcols]
            return hxc, zr, zi

        def recurrence(hd, hxc, zr, zi):
            hcols = slice(hd * hw, (hd + 1) * hw)
            tr = jnp.tanh(zr)
            ti = jnp.tanh(zi)
            a = jnp.exp2(tr * hkl[:, hcols] + hkl[:, hcols])
            q = 1.0 - a * a
            mult = q * jax.lax.rsqrt(jnp.maximum(q, 1e-30))
            bt = mult * (ti * hxc + hxc)
            a3 = a.reshape(tt, nb, hw)
            b3 = bt.reshape(tt, nb, hw)
            st = state_scr[:, hcols]
            for t in range(tt):
                st = a3[t] * st + b3[t]
                hs_scr[t, :, hcols] = st
            state_scr[:, hcols] = st

        def gate_out(hd, g_raw):
            hcols = slice(hd * hw, (hd + 1) * hw)
            y_scr[:, hcols] = hs_scr[:, :, hcols].reshape(tm, hw) * _silu(g_raw)

        G_LAG = 2
        g_raws = {}
        nxt = conv_gates(0)
        for hd in range(GROUPS):
            cur = nxt
            if hd + 2 < GROUPS:
                in_proj(hd + 2)
            if hd + 1 < GROUPS:
                nxt = conv_gates(hd + 1)
            if hd < len(late_tiles):
                y_tiles.append(out_tile(late_tiles[hd]))
                if hd == len(late_tiles) - 1:
                    out_stage(jnp.concatenate(y_tiles, axis=1))
            if hd - G_LAG >= 0:
                g_raws[hd - G_LAG] = _dot(h_scr[...], win_scr[GROUPS + hd - G_LAG])
            if hd == GROUPS - 1:
                for k in range(GROUPS - G_LAG, GROUPS):
                    g_raws[k] = _dot(h_scr[...], win_scr[GROUPS + k])
            recurrence(hd, *cur)
            if hd - G_LAG - 1 >= 0:
                gate_out(hd - G_LAG - 1, g_raws.pop(hd - G_LAG - 1))
        for hd in sorted(g_raws):
            gate_out(hd, g_raws[hd])

        xb_scr[0:pad] = xb_scr[tt:tt + pad]

    @pl.when(i == nsteps)
    def _():
        out_stage(out_proj())

    @pl.when(i >= 1)
    def _():
        for b in range(nb):
            _o_copy(obuf, o_hbm, sem_out, i - 1, (i - 1) % O_SLOTS, b, tt).start()

    @pl.when(i == nsteps)
    def _():
        for k in range(min(O_SLOTS, nsteps)):
            for b in range(nb):
                _o_copy(obuf, o_hbm, sem_out, i - 1 - k, (i - 1 - k) % O_SLOTS, b, tt).wait()


def _layer_b(x, mod, layer, widx, pre, post, w_in, conv_w, conv_b, ga_w, ga_b, gx_w, gx_b, lam,
             w_out, *, tt):
    nb, s, d = x.shape
    e = w_out.shape[1]
    hw = e // GROUPS
    tm = tt * nb
    nsteps = s // tt
    single = pl.Buffered(1)
    by_layer = lambda i: (layer, 0, 0)
    by_w = lambda i: (widx, 0, 0)
    by_w4 = lambda i: (widx, 0, 0, 0)
    whole2 = lambda i: (0, 0)
    kern = functools.partial(_layer_b_kernel, tt=tt, nb=nb, d=d, e=e, nsteps=nsteps, layer=layer,
                             widx=widx)
    return pl.pallas_call(
        kern,
        grid=(nsteps + 1,),
        in_specs=[
            pl.BlockSpec(memory_space=pl.ANY),
            pl.BlockSpec((None, nb, 3 * d), by_layer, pipeline_mode=single),
            pl.BlockSpec(pre.shape, whole2, pipeline_mode=single),
            pl.BlockSpec(post.shape, whole2, pipeline_mode=single),
            pl.BlockSpec(memory_space=pl.ANY),
            pl.BlockSpec((None, CONV_WIDTH, e), by_w, pipeline_mode=single),
            pl.BlockSpec(conv_b.shape, whole2, pipeline_mode=single),
            pl.BlockSpec((None, GROUPS, hw, hw), by_w4, pipeline_mode=single),
            pl.BlockSpec(ga_b.shape, whole2, pipeline_mode=single),
            pl.BlockSpec((None, GROUPS, hw, hw), by_w4, pipeline_mode=single),
            pl.BlockSpec(gx_b.shape, whole2, pipeline_mode=single),
            pl.BlockSpec(lam.shape, whole2, pipeline_mode=single),
            pl.BlockSpec(memory_space=pl.ANY),
        ],
        out_specs=pl.BlockSpec(memory_space=pl.ANY),
        out_shape=jax.ShapeDtypeStruct((nb, s, d), jnp.float32),
        scratch_shapes=[
            pltpu.VMEM((X_SLOTS, tt, nb, d), jnp.float32),
            pltpu.VMEM((O_SLOTS, tt, nb, d), jnp.float32),
            pltpu.SemaphoreType.DMA((X_SLOTS, nb)),
            pltpu.SemaphoreType.DMA((O_SLOTS, nb)),
            pltpu.VMEM((2 * e // MXU_N, d, MXU_N), jnp.bfloat16),
            pltpu.VMEM((d // MXU_N, e, MXU_N), jnp.bfloat16),
            pltpu.SemaphoreType.DMA((2 * e // MXU_N + d // MXU_N,)),
            pltpu.VMEM((tm, d), jnp.float32),
            pltpu.VMEM((tt + CONV_WIDTH - 1, nb, e), jnp.float32),
            pltpu.VMEM((tt, nb, e), jnp.float32),
            pltpu.VMEM((tm, e), jnp.float32),
            pltpu.VMEM((nb, e), jnp.float32),
        ],
        compiler_params=pltpu.CompilerParams(
            dimension_semantics=("arbitrary",),
            vmem_limit_bytes=VMEM_LIMIT_BYTES),
        name="rglru_layer",
    )(x, mod, pre, post, w_in, conv_w, conv_b, ga_w, ga_b, gx_w, gx_b, lam, w_out)


def kernel(x, c, mod_w, mod_b, pre_norm, post_norm, a_w_in, a_v_norm, a_w_s, a_b_s, a_w_out,
           b_w_in, b_conv_w, b_conv_b, b_ga_w, b_ga_b, b_gx_w, b_gx_b, b_lambda, b_w_out):
    depth = pre_norm.shape[0]
    bf16 = jnp.bfloat16
    mod = _mod_call(c, mod_w, mod_b)
    a_w_in, a_w_out = a_w_in.astype(bf16), a_w_out.astype(bf16)
    b_w_in, b_w_out = b_w_in.astype(bf16), b_w_out.astype(bf16)
    b_ga_w, b_gx_w = b_ga_w.astype(bf16), b_gx_w.astype(bf16)
    a_bs_lanes = jnp.broadcast_to(a_b_s[..., None], a_b_s.shape + (128,))
    for layer in range(depth):
        j = layer // 2
        if layer % 2 == 0:
            x = _layer_a(x, mod, layer, j, pre_norm, post_norm, a_w_in, a_v_norm, a_w_s,
                         a_bs_lanes, a_w_out, tm=512)
        else:
            x = _layer_b(x, mod, layer, j, pre_norm, post_norm, b_w_in, b_conv_w, b_conv_b,
                         b_ga_w, b_ga_b, b_gx_w, b_gx_b, b_lambda, b_w_out, tt=64)
    return x
```

```python
import functools
import math

import jax
import jax.numpy as jnp
from jax.experimental import pallas as pl
from jax.experimental.pallas import tpu as pltpu

EPS = 1e-6
CHUNK = 128
GROUPS = 8
CONV_WIDTH = 4
GATE_AHEAD = 2
LRU_C = 8.0
MXU_N = 256
VMEM_LIMIT_BYTES = 56 * 1024 * 1024

_GELU_C0 = math.sqrt(2.0 / math.pi)
_GELU_C1 = 0.044715
_LOG2E = 1.0 / math.log(2.0)


def _gelu_of_half(hx):
    inner = hx * (hx * hx * (8.0 * _GELU_C0 * _GELU_C1) + 2.0 * _GELU_C0)
    return hx * jnp.tanh(inner) + hx


def _silu_of_half(hx):
    return hx * jnp.tanh(hx) + hx


def _sigmoid(x):
    return 0.5 * jnp.tanh(0.5 * x) + 0.5


def _dot(a, b):
    return jnp.dot(a, b, preferred_element_type=jnp.float32)


def _weight_copies(w_hbm, widx, w_scr, sem, sem_base):
    n_tiles = w_scr.shape[0]
    return [pltpu.make_async_copy(w_hbm.at[widx, :, pl.ds(n * MXU_N, MXU_N)], w_scr.at[n],
                                  sem.at[sem_base + n]) for n in range(n_tiles)]


def _mod_kernel(c_ref, w_ref, b_ref, o_ref):
    c = c_ref[...]
    cond = c * _sigmoid(c)
    o_ref[...] = _dot(cond.astype(jnp.bfloat16), w_ref[...].astype(jnp.bfloat16)) + b_ref[...]


def _mod_call(c, mod_w, mod_b):
    depth, d, d3 = mod_w.shape
    b = c.shape[0]
    nb = d3 // d
    return pl.pallas_call(
        _mod_kernel,
        grid=(depth, nb),
        in_specs=[
            pl.BlockSpec((b, d), lambda l, n: (0, 0)),
            pl.BlockSpec((None, d, d), lambda l, n: (l, 0, n)),
            pl.BlockSpec((None, 1, d), lambda l, n: (l, 0, n)),
        ],
        out_specs=pl.BlockSpec((None, b, d), lambda l, n: (l, 0, n)),
        out_shape=jax.ShapeDtypeStruct((depth, b, d3), jnp.float32),
        compiler_params=pltpu.CompilerParams(
            dimension_semantics=("arbitrary", "arbitrary"),
            vmem_limit_bytes=VMEM_LIMIT_BYTES),
        name="adaln_mod",
    )(c, mod_w, mod_b.reshape(depth, 1, d3))


def _layer_a_kernel(xc_ref, xp_ref, mod_ref, pre_ref, post_ref, win_hbm, vnorm_ref,
                    ws_ref, bs_ref, wout_hbm, o_ref,
                    win_scr, wout_scr, wsem, h_scr, v_scr, y_scr, g_scr, yo_scr, wm_scr,
                    *, tm, d, e, nsteps, tpb, layer, widx):
    gw = e // GROUPS
    n_out = d // MXU_N
    n_rc = tm // CHUNK
    half = tm // 2
    i = pl.program_id(0)
    b_cur = jnp.minimum(i, nsteps - 1) // tpb
    b_prev = jnp.maximum(i - 1, 0) // tpb

    def rows(j):
        return slice(j * CHUNK, (j + 1) * CHUNK)

    def gcols(g):
        return slice(g * gw, (g + 1) * gw)

    @pl.when(i == 0)
    def _():
        copies = (_weight_copies(win_hbm, widx, win_scr, wsem, 0)
                  + _weight_copies(wout_hbm, widx, wout_scr, wsem, win_scr.shape[0]))
        for cp in copies:
            cp.start()
        row = jax.lax.broadcasted_iota(jnp.int32, (CHUNK, CHUNK), 0)
        col = jax.lax.broadcasted_iota(jnp.int32, (CHUNK, CHUNK), 1)
        for g in range(GROUPS):
            wm_scr[g] = jnp.where(col <= row, ws_ref[g], 0.0)
        y_scr[...] = jnp.zeros((tm, e), jnp.float32)
        for cp in copies:
            cp.wait()

    def out_proj():
        for n in range(n_out):
            yo_scr[:, n * MXU_N:(n + 1) * MXU_N] = _dot(y_scr[...], wout_scr[n])

    def out_piece(j):
        y = yo_scr[rows(j), :]
        gate = mod_ref[pl.ds(b_prev, 1), 2 * d:]
        ms2 = jnp.mean(y * y, axis=-1, keepdims=True)
        o_ref[rows(j), :] = (xp_ref[rows(j), :] + y * jax.lax.rsqrt(ms2 + EPS)
                             * (post_ref[layer:layer + 1, :] * gate))

    @pl.when(i < nsteps)
    def _():
        out_proj()

        x = xc_ref[...]
        shift, scale = mod_ref[pl.ds(b_cur, 1), :d], mod_ref[pl.ds(b_cur, 1), d:2 * d]
        ms = jnp.mean(x * x, axis=-1, keepdims=True)
        h_scr[...] = (x * jax.lax.rsqrt(ms + EPS) * (pre_ref[layer:layer + 1, :] * (1.0 + scale))
                      + shift)

        def proj(g):
            v_scr[:, gcols(g)] = _dot(h_scr[...], win_scr[GROUPS + g])
            y_scr[:, gcols(g)] = _dot(h_scr[...], win_scr[g])

        def gate_proj(g):
            g_scr[:, gcols(g)] = _dot(h_scr[...], win_scr[2 * GROUPS + g])

        s1 = [jnp.zeros((CHUNK, 1), jnp.float32) for _ in range(n_rc)]
        s2 = [jnp.zeros((CHUNK, 1), jnp.float32) for _ in range(n_rc)]
        proj(0)
        for g in range(GROUPS):
            if g + 1 < GROUPS:
                proj(g + 1)
            for j in range(n_rc):
                vc = _gelu_of_half(v_scr[rows(j), gcols(g)])
                v_scr[rows(j), gcols(g)] = vc
                s1[j] = s1[j] + jnp.sum(vc, axis=-1, keepdims=True)
                s2[j] = s2[j] + jnp.sum(vc * vc, axis=-1, keepdims=True)
                y_scr[rows(j), gcols(g)] = _gelu_of_half(y_scr[rows(j), gcols(g)])
            if g < n_rc:
                out_piece(g)
        for k in range(GATE_AHEAD):
            gate_proj(k)
        mean = [s * (1.0 / e) for s in s1]
        rstd = [jax.lax.rsqrt(q * (1.0 / e) - m * m + EPS) for q, m in zip(s2, mean)]

        def mix(g):
            for j in range(n_rc):
                vn = (v_scr[rows(j), gcols(g)] - mean[j]) * rstd[j] * vnorm_ref[widx:widx + 1, gcols(g)]
                v_scr[rows(j), gcols(g)] = _dot(wm_scr[g], vn.astype(jnp.bfloat16))

        mix(0)
        for g in range(GROUPS):
            if g + GATE_AHEAD < GROUPS:
                gate_proj(g + GATE_AHEAD)
            if g + 1 < GROUPS:
                mix(g + 1)
            bias = jnp.concatenate([bs_ref[g]] * (half // CHUNK), axis=0)
            bias = jnp.concatenate([bias] * (gw // 128), axis=1)
            for k in range(2):
                hr = slice(k * half, (k + 1) * half)
                y_scr[hr, gcols(g)] = (y_scr[hr, gcols(g)] * (v_scr[hr, gcols(g)] + bias)
                                       * _silu_of_half(g_scr[hr, gcols(g)]))

    @pl.when(i == nsteps)
    def _():
        out_proj()
        for j in range(n_rc):
            out_piece(j)


def _layer_a(x, mod, layer, widx, pre, post, w_in, v_norm, w_s, bs_lanes, w_out, *, tm):
    b, s, d = x.shape
    e = w_out.shape[1]
    tpb = s // tm
    nsteps = b * tpb
    cur = lambda i: jnp.minimum(i, nsteps - 1)
    prev = lambda i: jnp.maximum(i - 1, 0)
    single = pl.Buffered(1)
    whole2 = lambda i: (0, 0)
    kern = functools.partial(_layer_a_kernel, tm=tm, d=d, e=e, nsteps=nsteps, tpb=tpb,
                             layer=layer, widx=widx)
    out = pl.pallas_call(
        kern,
        grid=(nsteps + 1,),
        in_specs=[
            pl.BlockSpec((tm, d), lambda i: (cur(i), 0)),
            pl.BlockSpec((tm, d), lambda i: (prev(i), 0)),
            pl.BlockSpec((None, b, 3 * d), lambda i: (layer, 0, 0), pipeline_mode=single),
            pl.BlockSpec(pre.shape, whole2, pipeline_mode=single),
            pl.BlockSpec(post.shape, whole2, pipeline_mode=single),
            pl.BlockSpec(memory_space=pl.ANY),
            pl.BlockSpec(v_norm.shape, whole2, pipeline_mode=single),
            pl.BlockSpec((None, GROUPS, CHUNK, CHUNK), lambda i: (widx, 0, 0, 0),
                         pipeline_mode=single),
            pl.BlockSpec((None, GROUPS, CHUNK, 128), lambda i: (widx, 0, 0, 0),
                         pipeline_mode=single),
            pl.BlockSpec(memory_space=pl.ANY),
        ],
        out_specs=pl.BlockSpec((tm, d), lambda i: (prev(i), 0)),
        out_shape=jax.ShapeDtypeStruct((b * s, d), jnp.float32),
        scratch_shapes=[
            pltpu.VMEM((3 * e // MXU_N, d, MXU_N), jnp.bfloat16),
            pltpu.VMEM((d // MXU_N, e, MXU_N), jnp.bfloat16),
            pltpu.SemaphoreType.DMA((3 * e // MXU_N + d // MXU_N,)),
            pltpu.VMEM((tm, d), jnp.float32),
            pltpu.VMEM((tm, e), jnp.float32),
            pltpu.VMEM((tm, e), jnp.float32),
            pltpu.VMEM((tm, e), jnp.float32),
            pltpu.VMEM((tm, d), jnp.float32),
            pltpu.VMEM((GROUPS, CHUNK, CHUNK), jnp.float32),
        ],
        compiler_params=pltpu.CompilerParams(
            dimension_semantics=("arbitrary",),
            vmem_limit_bytes=VMEM_LIMIT_BYTES),
        name="gmlp_layer",
    )(x.reshape(b * s, d), x.reshape(b * s, d), mod, pre, post, w_in, v_norm, w_s, bs_lanes,
      w_out)
    return out.reshape(b, s, d)


X_SLOTS = 3
O_SLOTS = 2


def _x_copy(x_hbm, xbuf, sem, step, slot, b, tt):
    return pltpu.make_async_copy(x_hbm.at[b, pl.ds(step * tt, tt), :], xbuf.at[slot, :, b, :],
                                 sem.at[slot, b])


def _o_copy(obuf, o_hbm, sem, step, slot, b, tt):
    return pltpu.make_async_copy(obuf.at[slot, :, b, :], o_hbm.at[b, pl.ds(step * tt, tt), :],
                                 sem.at[slot, b])


def _layer_b_kernel(x_hbm, mod_ref, pre_ref, post_ref, win_hbm, cw_ref, cb_ref, gaw_ref,
                    gab_ref, gxw_ref, gxb_ref, lam_ref, wout_hbm, o_hbm,
                    xbuf, obuf, sem_in, sem_out, win_scr, wout_scr, wsem,
                    h_scr, xb_scr, hs_scr, y_scr, state_scr,
                    *, tt, nb, d, e, nsteps, layer, widx):
    hw = e // GROUPS
    tm = tt * nb
    pad = CONV_WIDTH - 1
    n_out = d // MXU_N
    i = pl.program_id(0)

    @pl.when(i == 0)
    def _():
        copies = (_weight_copies(win_hbm, widx, win_scr, wsem, 0)
                  + _weight_copies(wout_hbm, widx, wout_scr, wsem, win_scr.shape[0]))
        for cp in copies:
            cp.start()
        for b in range(nb):
            _x_copy(x_hbm, xbuf, sem_in, 0, 0, b, tt).start()
        xb_scr[0:pad] = jnp.zeros((pad, nb, e), jnp.float32)
        state_scr[...] = jnp.zeros((nb, e), jnp.float32)
        y_scr[...] = jnp.zeros((tm, e), jnp.float32)
        xbuf[X_SLOTS - 1] = jnp.zeros((tt, nb, d), jnp.float32)
        for cp in copies:
            cp.wait()

    @pl.when(i + 1 < nsteps)
    def _():
        for b in range(nb):
            _x_copy(x_hbm, xbuf, sem_in, i + 1, (i + 1) % X_SLOTS, b, tt).start()

    @pl.when(i < nsteps)
    def _():
        for b in range(nb):
            _x_copy(x_hbm, xbuf, sem_in, i, i % X_SLOTS, b, tt).wait()

    @pl.when(i >= O_SLOTS + 1)
    def _():
        for b in range(nb):
            _o_copy(obuf, o_hbm, sem_out, i - 1 - O_SLOTS, (i - 1) % O_SLOTS, b, tt).wait()

    def out_tile(n):
        return _dot(y_scr[...], wout_scr[n])

    def out_proj():
        return jnp.concatenate([out_tile(n) for n in range(n_out)], axis=1)

    def out_stage(y):
        gate = mod_ref[:, 2 * d:]
        xp = xbuf[(i + X_SLOTS - 1) % X_SLOTS]
        y = y.reshape(tt, nb, d)
        ms2 = jnp.mean(y * y, axis=-1, keepdims=True)
        obuf[(i + O_SLOTS - 1) % O_SLOTS] = (
            xp + y * jax.lax.rsqrt(ms2 + EPS) * (post_ref[layer:layer + 1, :] * gate)[None])

    @pl.when(i < nsteps)
    def _():
        early_tiles, late_tiles = (0, 1), (2, 3)
        y_tiles = [out_tile(n) for n in early_tiles]

        x = xbuf[i % X_SLOTS]
        shift, scale = mod_ref[:, :d], mod_ref[:, d:2 * d]
        ms = jnp.mean(x * x, axis=-1, keepdims=True)
        h = (x * jax.lax.rsqrt(ms + EPS) * (pre_ref[layer:layer + 1, :] * (1.0 + scale))[None]
             + shift[None])
        h_scr[...] = h.reshape(tm, d)

        def in_proj(hd):
            hcols = slice(hd * hw, (hd + 1) * hw)
            xb_scr[pad:pad + tt, :, hcols] = _dot(h_scr[...], win_scr[hd]).reshape(tt, nb, hw)

        in_proj(0)
        in_proj(1)

        lam = lam_ref[widx:widx + 1, :]
        hkl = (-0.5 * LRU_C * _LOG2E) * (
            jnp.maximum(-lam, 0.0) + jnp.log(1.0 + jnp.exp(-jnp.abs(lam))))
        hcw = 0.5 * cw_ref[...]
        hcb = 0.5 * cb_ref[widx:widx + 1, :]
        hgab = 0.5 * gab_ref[widx:widx + 1, :]
        hgxb = 0.5 * gxb_ref[widx:widx + 1, :]

        def conv(hd):
            hcols = slice(hd * hw, (hd + 1) * hw)
            hxc = hcb[:, hcols][None] + xb_scr[0:tt, :, hcols] * hcw[0:1, hcols][None]
            for k in range(1, CONV_WIDTH):
                hxc = hxc + xb_scr[k:k + tt, :, hcols] * hcw[k:k + 1, hcols][None]
            return hxc.reshape(tm, hw)

        def gates(hd, hxc):
            hcols = slice(hd * hw, (hd + 1) * hw)
            zr = _dot(hxc, gaw_ref[hd]) + hgab[:, hcols]
            zi = _dot(hxc, gxw_ref[hd]) + hgxb[:, hcols]
            return hxc, zr, zi

        def recurrence(hd, hxc, zr, zi):
            hcols = slice(hd * hw, (hd + 1) * hw)
            tr = jnp.tanh(zr)
            ti = jnp.tanh(zi)
            a = jnp.exp2(tr * hkl[:, hcols] + hkl[:, hcols])
            q = 1.0 - a * a
            mult = q * jax.lax.rsqrt(jnp.maximum(q, 1e-30))
            bt = mult * (ti * hxc + hxc)
            a3 = a.reshape(tt, nb, hw)
            b3 = bt.reshape(tt, nb, hw)
            st = state_scr[:, hcols]
            for t in range(tt):
                st = a3[t] * st + b3[t]
                hs_scr[t, :, hcols] = st
            state_scr[:, hcols] = st

        def gate_out(hd, g_raw):
            hcols = slice(hd * hw, (hd + 1) * hw)
            y_scr[:, hcols] = hs_scr[:, :, hcols].reshape(tm, hw) * _silu_of_half(g_raw)

        G_LAG = 2
        g_raws = {}
        nxt = gates(0, conv(0))
        for hd in range(GROUPS):
            cur = nxt
            if hd + 1 < GROUPS:
                hxc_nxt = conv(hd + 1)
            if hd + 2 < GROUPS:
                in_proj(hd + 2)
            if hd < len(late_tiles):
                y_tiles.append(out_tile(late_tiles[hd]))
                if hd == len(late_tiles) - 1:
                    out_stage(jnp.concatenate(y_tiles, axis=1))
            if hd - G_LAG >= 0:
                g_raws[hd - G_LAG] = _dot(h_scr[...], win_scr[GROUPS + hd - G_LAG])
            if hd + 1 < GROUPS:
                nxt = gates(hd + 1, hxc_nxt)
            if hd == GROUPS - 1:
                for k in range(GROUPS - G_LAG, GROUPS):
                    g_raws[k] = _dot(h_scr[...], win_scr[GROUPS + k])
            recurrence(hd, *cur)
            if hd - G_LAG - 1 >= 0:
                gate_out(hd - G_LAG - 1, g_raws.pop(hd - G_LAG - 1))
        for hd in sorted(g_raws):
            gate_out(hd, g_raws[hd])

        xb_scr[0:pad] = xb_scr[tt:tt + pad]

    @pl.when(i == nsteps)
    def _():
        out_stage(out_proj())

    @pl.when(i >= 1)
    def _():
        for b in range(nb):
            _o_copy(obuf, o_hbm, sem_out, i - 1, (i - 1) % O_SLOTS, b, tt).start()

    @pl.when(i == nsteps)
    def _():
        for k in range(min(O_SLOTS, nsteps)):
            for b in range(nb):
                _o_copy(obuf, o_hbm, sem_out, i - 1 - k, (i - 1 - k) % O_SLOTS, b, tt).wait()


def _layer_b(x, mod, layer, widx, pre, post, w_in, conv_w, conv_b, ga_w, ga_b, gx_w, gx_b, lam,
             w_out, *, tt):
    nb, s, d = x.shape
    e = w_out.shape[1]
    hw = e // GROUPS
    tm = tt * nb
    nsteps = s // tt
    single = pl.Buffered(1)
    by_layer = lambda i: (layer, 0, 0)
    by_w = lambda i: (widx, 0, 0)
    by_w4 = lambda i: (widx, 0, 0, 0)
    whole2 = lambda i: (0, 0)
    kern = functools.partial(_layer_b_kernel, tt=tt, nb=nb, d=d, e=e, nsteps=nsteps, layer=layer,
                             widx=widx)
    return pl.pallas_call(
        kern,
        grid=(nsteps + 1,),
        in_specs=[
            pl.BlockSpec(memory_space=pl.ANY),
            pl.BlockSpec((None, nb, 3 * d), by_layer, pipeline_mode=single),
            pl.BlockSpec(pre.shape, whole2, pipeline_mode=single),
            pl.BlockSpec(post.shape, whole2, pipeline_mode=single),
            pl.BlockSpec(memory_space=pl.ANY),
            pl.BlockSpec((None, CONV_WIDTH, e), by_w, pipeline_mode=single),
            pl.BlockSpec(conv_b.shape, whole2, pipeline_mode=single),
            pl.BlockSpec((None, GROUPS, hw, hw), by_w4, pipeline_mode=single),
            pl.BlockSpec(ga_b.shape, whole2, pipeline_mode=single),
            pl.BlockSpec((None, GROUPS, hw, hw), by_w4, pipeline_mode=single),
            pl.BlockSpec(gx_b.shape, whole2, pipeline_mode=single),
            pl.BlockSpec(lam.shape, whole2, pipeline_mode=single),
            pl.BlockSpec(memory_space=pl.ANY),
        ],
        out_specs=pl.BlockSpec(memory_space=pl.ANY),
        out_shape=jax.ShapeDtypeStruct((nb, s, d), jnp.float32),
        scratch_shapes=[
            pltpu.VMEM((X_SLOTS, tt, nb, d), jnp.float32),
            pltpu.VMEM((O_SLOTS, tt, nb, d), jnp.float32),
            pltpu.SemaphoreType.DMA((X_SLOTS, nb)),
            pltpu.SemaphoreType.DMA((O_SLOTS, nb)),
            pltpu.VMEM((2 * e // MXU_N, d, MXU_N), jnp.bfloat16),
            pltpu.VMEM((d // MXU_N, e, MXU_N), jnp.bfloat16),
            pltpu.SemaphoreType.DMA((2 * e // MXU_N + d // MXU_N,)),
            pltpu.VMEM((tm, d), jnp.float32),
            pltpu.VMEM((tt + CONV_WIDTH - 1, nb, e), jnp.float32),
            pltpu.VMEM((tt, nb, e), jnp.float32),
            pltpu.VMEM((tm, e), jnp.float32),
            pltpu.VMEM((nb, e), jnp.float32),
        ],
        compiler_params=pltpu.CompilerParams(
            dimension_semantics=("arbitrary",),
            vmem_limit_bytes=VMEM_LIMIT_BYTES),
        name="rglru_layer",
    )(x, mod, pre, post, w_in, conv_w, conv_b, ga_w, ga_b, gx_w, gx_b, lam, w_out)


def kernel(x, c, mod_w, mod_b, pre_norm, post_norm, a_w_in, a_v_norm, a_w_s, a_b_s, a_w_out,
           b_w_in, b_conv_w, b_conv_b, b_ga_w, b_ga_b, b_gx_w, b_gx_b, b_lambda, b_w_out):
    depth = pre_norm.shape[0]
    bf16 = jnp.bfloat16
    mod = _mod_call(c, mod_w, mod_b)
    e = b_w_out.shape[1]
    b_in_scale = jnp.where(jnp.arange(2 * e) < e, 1.0, 0.5).astype(b_w_in.dtype)
    a_w_in, a_w_out = (0.5 * a_w_in).astype(bf16), a_w_out.astype(bf16)
    b_w_in, b_w_out = (b_w_in * b_in_scale).astype(bf16), b_w_out.astype(bf16)
    b_ga_w, b_gx_w = b_ga_w.astype(bf16), b_gx_w.astype(bf16)
    a_bs_lanes = jnp.broadcast_to(a_b_s[..., None], a_b_s.shape + (128,))
    for layer in range(depth):
        j = layer // 2
        if layer % 2 == 0:
            x = _layer_a(x, mod, layer, j, pre_norm, post_norm, a_w_in, a_v_norm, a_w_s,
                         a_bs_lanes, a_w_out, tm=512)
        else:
            x = _layer_b(x, mod, layer, j, pre_norm, post_norm, b_w_in, b_conv_w, b_conv_b,
                         b_ga_w, b_ga_b, b_gx_w, b_gx_b, b_lambda, b_w_out, tt=64)
    return x
```

```python
import functools
import math

import jax
import jax.numpy as jnp
from jax.experimental import pallas as pl
from jax.experimental.pallas import tpu as pltpu

EPS = 1e-6
CHUNK = 128
GROUPS = 8
CONV_WIDTH = 4
GATE_AHEAD = 2
LRU_C = 8.0
MXU_N = 256
VMEM_LIMIT_BYTES = 56 * 1024 * 1024

_GELU_C0 = math.sqrt(2.0 / math.pi)
_GELU_C1 = 0.044715
_LOG2E = 1.0 / math.log(2.0)


def _gelu_of_half(hx):
    inner = hx * (hx * hx * (8.0 * _GELU_C0 * _GELU_C1) + 2.0 * _GELU_C0)
    return hx * jnp.tanh(inner) + hx


def _silu_of_half(hx):
    return hx * jnp.tanh(hx) + hx


def _sigmoid(x):
    return 0.5 * jnp.tanh(0.5 * x) + 0.5


def _dot(a, b):
    return jnp.dot(a, b, preferred_element_type=jnp.float32)


def _weight_copies(w_hbm, widx, w_scr, sem, sem_base):
    n_tiles = w_scr.shape[0]
    return [pltpu.make_async_copy(w_hbm.at[widx, :, pl.ds(n * MXU_N, MXU_N)], w_scr.at[n],
                                  sem.at[sem_base + n]) for n in range(n_tiles)]


def _mod_kernel(c_ref, w_ref, b_ref, o_ref):
    c = c_ref[...]
    cond = c * _sigmoid(c)
    bias = b_ref[pl.ds(pl.program_id(0), 1), :]
    o_ref[...] = _dot(cond.astype(jnp.bfloat16), w_ref[...].astype(jnp.bfloat16)) + bias


def _mod_call(c, mod_w, mod_b):
    depth, d, d3 = mod_w.shape
    b = c.shape[0]
    return pl.pallas_call(
        _mod_kernel,
        grid=(depth,),
        in_specs=[
            pl.BlockSpec((b, d), lambda l: (0, 0)),
            pl.BlockSpec((None, d, d3), lambda l: (l, 0, 0)),
            pl.BlockSpec((depth, d3), lambda l: (0, 0)),
        ],
        out_specs=pl.BlockSpec((None, b, d3), lambda l: (l, 0, 0)),
        out_shape=jax.ShapeDtypeStruct((depth, b, d3), jnp.float32),
        compiler_params=pltpu.CompilerParams(
            dimension_semantics=("arbitrary",),
            vmem_limit_bytes=VMEM_LIMIT_BYTES),
        name="adaln_mod",
    )(c, mod_w, mod_b)


def _layer_a_kernel(xc_ref, xp_ref, mod_ref, pre_ref, post_ref, win_hbm, vnorm_ref,
                    ws_ref, bs_ref, wout_hbm, o_ref,
                    win_scr, wout_scr, wsem, h_scr, v_scr, y_scr, g_scr, yo_scr, wm_scr,
                    *, tm, d, e, nsteps, tpb, layer, widx):
    gw = e // GROUPS
    n_out = d // MXU_N
    n_rc = tm // CHUNK
    half = tm // 2
    i = pl.program_id(0)
    b_cur = jnp.minimum(i, nsteps - 1) // tpb
    b_prev = jnp.maximum(i - 1, 0) // tpb

    def rows(j):
        return slice(j * CHUNK, (j + 1) * CHUNK)

    def gcols(g):
        return slice(g * gw, (g + 1) * gw)

    @pl.when(i == 0)
    def _():
        copies = (_weight_copies(win_hbm, widx, win_scr, wsem, 0)
                  + _weight_copies(wout_hbm, widx, wout_scr, wsem, win_scr.shape[0]))
        for cp in copies:
            cp.start()
        row = jax.lax.broadcasted_iota(jnp.int32, (CHUNK, CHUNK), 0)
        col = jax.lax.broadcasted_iota(jnp.int32, (CHUNK, CHUNK), 1)
        for g in range(GROUPS):
            wm_scr[g] = jnp.where(col <= row, ws_ref[g], 0.0)
        y_scr[...] = jnp.zeros((tm, e), jnp.float32)
        for cp in copies:
            cp.wait()

    def out_proj():
        for n in range(n_out):
            yo_scr[:, n * MXU_N:(n + 1) * MXU_N] = _dot(y_scr[...], wout_scr[n])

    def out_piece(j):
        y = yo_scr[rows(j), :]
        gate = mod_ref[pl.ds(b_prev, 1), 2 * d:]
        ms2 = jnp.mean(y * y, axis=-1, keepdims=True)
        o_ref[rows(j), :] = (xp_ref[rows(j), :] + y * jax.lax.rsqrt(ms2 + EPS)
                             * (post_ref[layer:layer + 1, :] * gate))

    @pl.when(i < nsteps)
    def _():
        out_proj()

        x = xc_ref[...]
        shift, scale = mod_ref[pl.ds(b_cur, 1), :d], mod_ref[pl.ds(b_cur, 1), d:2 * d]
        ms = jnp.mean(x * x, axis=-1, keepdims=True)
        h_scr[...] = (x * jax.lax.rsqrt(ms + EPS) * (pre_ref[layer:layer + 1, :] * (1.0 + scale))
                      + shift)

        def proj(g):
            v_scr[:, gcols(g)] = _dot(h_scr[...], win_scr[GROUPS + g])
            y_scr[:, gcols(g)] = _dot(h_scr[...], win_scr[g])

        def gate_proj(g):
            g_scr[:, gcols(g)] = _dot(h_scr[...], win_scr[2 * GROUPS + g])

        s1 = [jnp.zeros((CHUNK, 1), jnp.float32) for _ in range(n_rc)]
        s2 = [jnp.zeros((CHUNK, 1), jnp.float32) for _ in range(n_rc)]
        proj(0)
        for g in range(GROUPS):
            if g + 1 < GROUPS:
                proj(g + 1)
            for j in range(n_rc):
                vc = _gelu_of_half(v_scr[rows(j), gcols(g)])
                v_scr[rows(j), gcols(g)] = vc
                s1[j] = s1[j] + jnp.sum(vc, axis=-1, keepdims=True)
                s2[j] = s2[j] + jnp.sum(vc * vc, axis=-1, keepdims=True)
                y_scr[rows(j), gcols(g)] = _gelu_of_half(y_scr[rows(j), gcols(g)])
            if g < n_rc:
                out_piece(g)
        for k in range(GATE_AHEAD):
            gate_proj(k)
        mean = [s * (1.0 / e) for s in s1]
        rstd = [jax.lax.rsqrt(q * (1.0 / e) - m * m + EPS) for q, m in zip(s2, mean)]

        def mix(g):
            for j in range(n_rc):
                vn = (v_scr[rows(j), gcols(g)] - mean[j]) * rstd[j] * vnorm_ref[widx:widx + 1, gcols(g)]
                v_scr[rows(j), gcols(g)] = _dot(wm_scr[g], vn.astype(jnp.bfloat16))

        mix(0)
        for g in range(GROUPS):
            if g + GATE_AHEAD < GROUPS:
                gate_proj(g + GATE_AHEAD)
            if g + 1 < GROUPS:
                mix(g + 1)
            bias = jnp.concatenate([bs_ref[g]] * (half // CHUNK), axis=0)
            bias = jnp.concatenate([bias] * (gw // 128), axis=1)
            for k in range(2):
                hr = slice(k * half, (k + 1) * half)
                y_scr[hr, gcols(g)] = (y_scr[hr, gcols(g)] * (v_scr[hr, gcols(g)] + bias)
                                       * _silu_of_half(g_scr[hr, gcols(g)]))

    @pl.when(i == nsteps)
    def _():
        out_proj()
        for j in range(n_rc):
            out_piece(j)


def _layer_a(x, mod, layer, widx, pre, post, w_in, v_norm, w_s, bs_lanes, w_out, *, tm):
    b, s, d = x.shape
    e = w_out.shape[1]
    tpb = s // tm
    nsteps = b * tpb
    cur = lambda i: jnp.minimum(i, nsteps - 1)
    prev = lambda i: jnp.maximum(i - 1, 0)
    single = pl.Buffered(1)
    whole2 = lambda i: (0, 0)
    kern = functools.partial(_layer_a_kernel, tm=tm, d=d, e=e, nsteps=nsteps, tpb=tpb,
                             layer=layer, widx=widx)
    out = pl.pallas_call(
        kern,
        grid=(nsteps + 1,),
        in_specs=[
            pl.BlockSpec((tm, d), lambda i: (cur(i), 0)),
            pl.BlockSpec((tm, d), lambda i: (prev(i), 0)),
            pl.BlockSpec((None, b, 3 * d), lambda i: (layer, 0, 0), pipeline_mode=single),
            pl.BlockSpec(pre.shape, whole2, pipeline_mode=single),
            pl.BlockSpec(post.shape, whole2, pipeline_mode=single),
            pl.BlockSpec(memory_space=pl.ANY),
            pl.BlockSpec(v_norm.shape, whole2, pipeline_mode=single),
            pl.BlockSpec((None, GROUPS, CHUNK, CHUNK), lambda i: (widx, 0, 0, 0),
                         pipeline_mode=single),
            pl.BlockSpec((None, GROUPS, CHUNK, 128), lambda i: (widx, 0, 0, 0),
                         pipeline_mode=single),
            pl.BlockSpec(memory_space=pl.ANY),
        ],
        out_specs=pl.BlockSpec((tm, d), lambda i: (prev(i), 0)),
        out_shape=jax.ShapeDtypeStruct((b * s, d), jnp.float32),
        scratch_shapes=[
            pltpu.VMEM((3 * e // MXU_N, d, MXU_N), jnp.bfloat16),
            pltpu.VMEM((d // MXU_N, e, MXU_N), jnp.bfloat16),
            pltpu.SemaphoreType.DMA((3 * e // MXU_N + d // MXU_N,)),
            pltpu.VMEM((tm, d), jnp.float32),
            pltpu.VMEM((tm, e), jnp.float32),
            pltpu.VMEM((tm, e), jnp.float32),
            pltpu.VMEM((tm, e), jnp.float32),
            pltpu.VMEM((tm, d), jnp.float32),
            pltpu.VMEM((GROUPS, CHUNK, CHUNK), jnp.float32),
        ],
        compiler_params=pltpu.CompilerParams(
            dimension_semantics=("arbitrary",),
            vmem_limit_bytes=VMEM_LIMIT_BYTES),
        name="gmlp_layer",
    )(x.reshape(b * s, d), x.reshape(b * s, d), mod, pre, post, w_in, v_norm, w_s, bs_lanes,
      w_out)
    return out.reshape(b, s, d)


X_SLOTS = 3
O_SLOTS = 2


def _x_copy(x_hbm, xbuf, sem, step, slot, b, tt):
    return pltpu.make_async_copy(x_hbm.at[b, pl.ds(step * tt, tt), :], xbuf.at[slot, :, b, :],
                                 sem.at[slot, b])


def _o_copy(obuf, o_hbm, sem, step, slot, b, tt):
    return pltpu.make_async_copy(obuf.at[slot, :, b, :], o_hbm.at[b, pl.ds(step * tt, tt), :],
                                 sem.at[slot, b])


def _layer_b_kernel(x_hbm, mod_ref, pre_ref, post_ref, win_hbm, cw_ref, cb_ref, gaw_ref,
                    gab_ref, gxw_ref, gxb_ref, lam_ref, wout_hbm, o_hbm,
                    xbuf, obuf, sem_in, sem_out, win_scr, wout_scr, wsem,
                    h_scr, xb_scr, hs_scr, y_scr, state_scr,
                    *, tt, nb, d, e, nsteps, layer, widx):
    hw = e // GROUPS
    tm = tt * nb
    pad = CONV_WIDTH - 1
    n_out = d // MXU_N
    i = pl.program_id(0)

    @pl.when(i == 0)
    def _():
        copies = (_weight_copies(win_hbm, widx, win_scr, wsem, 0)
                  + _weight_copies(wout_hbm, widx, wout_scr, wsem, win_scr.shape[0]))
        for cp in copies:
            cp.start()
        for b in range(nb):
            _x_copy(x_hbm, xbuf, sem_in, 0, 0, b, tt).start()
        xb_scr[0:pad] = jnp.zeros((pad, nb, e), jnp.float32)
        state_scr[...] = jnp.zeros((nb, e), jnp.float32)
        y_scr[...] = jnp.zeros((tm, e), jnp.float32)
        xbuf[X_SLOTS - 1] = jnp.zeros((tt, nb, d), jnp.float32)
        for cp in copies:
            cp.wait()

    @pl.when(i + 1 < nsteps)
    def _():
        for b in range(nb):
            _x_copy(x_hbm, xbuf, sem_in, i + 1, (i + 1) % X_SLOTS, b, tt).start()

    @pl.when(i < nsteps)
    def _():
        for b in range(nb):
            _x_copy(x_hbm, xbuf, sem_in, i, i % X_SLOTS, b, tt).wait()

    @pl.when(i >= O_SLOTS + 1)
    def _():
        for b in range(nb):
            _o_copy(obuf, o_hbm, sem_out, i - 1 - O_SLOTS, (i - 1) % O_SLOTS, b, tt).wait()

    def out_tile(n):
        return _dot(y_scr[...], wout_scr[n])

    def out_proj():
        return jnp.concatenate([out_tile(n) for n in range(n_out)], axis=1)

    def out_stage(y):
        gate = mod_ref[:, 2 * d:]
        xp = xbuf[(i + X_SLOTS - 1) % X_SLOTS]
        y = y.reshape(tt, nb, d)
        ms2 = jnp.mean(y * y, axis=-1, keepdims=True)
        obuf[(i + O_SLOTS - 1) % O_SLOTS] = (
            xp + y * jax.lax.rsqrt(ms2 + EPS) * (post_ref[layer:layer + 1, :] * gate)[None])

    @pl.when(i < nsteps)
    def _():
        early_tiles, late_tiles = (0, 1), (2, 3)
        y_tiles = [out_tile(n) for n in early_tiles]

        x = xbuf[i % X_SLOTS]
        shift, scale = mod_ref[:, :d], mod_ref[:, d:2 * d]
        ms = jnp.mean(x * x, axis=-1, keepdims=True)
        h = (x * jax.lax.rsqrt(ms + EPS) * (pre_ref[layer:layer + 1, :] * (1.0 + scale))[None]
             + shift[None])
        h_scr[...] = h.reshape(tm, d)

        def in_proj(hd):
            hcols = slice(hd * hw, (hd + 1) * hw)
            xb_scr[pad:pad + tt, :, hcols] = _dot(h_scr[...], win_scr[hd]).reshape(tt, nb, hw)

        in_proj(0)
        in_proj(1)

        lam = lam_ref[widx:widx + 1, :]
        hkl = (-0.5 * LRU_C * _LOG2E) * (
            jnp.maximum(-lam, 0.0) + jnp.log(1.0 + jnp.exp(-jnp.abs(lam))))
        hcw = 0.5 * cw_ref[...]
        hcb = 0.5 * cb_ref[widx:widx + 1, :]
        hgab = 0.5 * gab_ref[widx:widx + 1, :]
        hgxb = 0.5 * gxb_ref[widx:widx + 1, :]

        def conv_gates(hd):
            hcols = slice(hd * hw, (hd + 1) * hw)
            hxc = hcb[:, hcols][None] + xb_scr[0:tt, :, hcols] * hcw[0:1, hcols][None]
            for k in range(1, CONV_WIDTH):
                hxc = hxc + xb_scr[k:k + tt, :, hcols] * hcw[k:k + 1, hcols][None]
            hxc = hxc.reshape(tm, hw)
            zr = _dot(hxc, gaw_ref[hd]) + hgab[:, hcols]
            zi = _dot(hxc, gxw_ref[hd]) + hgxb[:, hcols]
            return hxc, zr, zi

        def recurrence(hd, hxc, zr, zi):
            hcols = slice(hd * hw, (hd + 1) * hw)
            tr = jnp.tanh(zr)
            ti = jnp.tanh(zi)
            a = jnp.exp2(tr * hkl[:, hcols] + hkl[:, hcols])
            q = 1.0 - a * a
            mult = q * jax.lax.rsqrt(jnp.maximum(q, 1e-30))
            bt = mult * (ti * hxc + hxc)
            a3 = a.reshape(tt, nb, hw)
            b3 = bt.reshape(tt, nb, hw)
            st = state_scr[:, hcols]
            for t in range(tt):
                st = a3[t] * st + b3[t]
                hs_scr[t, :, hcols] = st
            state_scr[:, hcols] = st

        def gate_out(hd, g_raw):
            hcols = slice(hd * hw, (hd + 1) * hw)
            y_scr[:, hcols] = hs_scr[:, :, hcols].reshape(tm, hw) * _silu_of_half(g_raw)

        G_LAG = 2
        g_raws = {}
        nxt = conv_gates(0)
        for hd in range(GROUPS):
            cur = nxt
            if hd + 2 < GROUPS:
                in_proj(hd + 2)
            if hd + 1 < GROUPS:
                nxt = conv_gates(hd + 1)
            if hd < len(late_tiles):
                y_tiles.append(out_tile(late_tiles[hd]))
                if hd == len(late_tiles) - 1:
                    out_stage(jnp.concatenate(y_tiles, axis=1))
            if hd - G_LAG >= 0:
                g_raws[hd - G_LAG] = _dot(h_scr[...], win_scr[GROUPS + hd - G_LAG])
            if hd == GROUPS - 1:
                for k in range(GROUPS - G_LAG, GROUPS):
                    g_raws[k] = _dot(h_scr[...], win_scr[GROUPS + k])
            recurrence(hd, *cur)
            if hd - G_LAG - 1 >= 0:
                gate_out(hd - G_LAG - 1, g_raws.pop(hd - G_LAG - 1))
        for hd in sorted(g_raws):
            gate_out(hd, g_raws[hd])

        xb_scr[0:pad] = xb_scr[tt:tt + pad]

    @pl.when(i == nsteps)
    def _():
        out_stage(out_proj())

    @pl.when(i >= 1)
    def _():
        for b in range(nb):
            _o_copy(obuf, o_hbm, sem_out, i - 1, (i - 1) % O_SLOTS, b, tt).start()

    @pl.when(i == nsteps)
    def _():
        for k in range(min(O_SLOTS, nsteps)):
            for b in range(nb):
                _o_copy(obuf, o_hbm, sem_out, i - 1 - k, (i - 1 - k) % O_SLOTS, b, tt).wait()


def _layer_b(x, mod, layer, widx, pre, post, w_in, conv_w, conv_b, ga_w, ga_b, gx_w, gx_b, lam,
             w_out, *, tt):
    nb, s, d = x.shape
    e = w_out.shape[1]
    hw = e // GROUPS
    tm = tt * nb
    nsteps = s // tt
    single = pl.Buffered(1)
    by_layer = lambda i: (layer, 0, 0)
    by_w = lambda i: (widx, 0, 0)
    by_w4 = lambda i: (widx, 0, 0, 0)
    whole2 = lambda i: (0, 0)
    kern = functools.partial(_layer_b_kernel, tt=tt, nb=nb, d=d, e=e, nsteps=nsteps, layer=layer,
                             widx=widx)
    return pl.pallas_call(
        kern,
        grid=(nsteps + 1,),
        in_specs=[
            pl.BlockSpec(memory_space=pl.ANY),
            pl.BlockSpec((None, nb, 3 * d), by_layer, pipeline_mode=single),
            pl.BlockSpec(pre.shape, whole2, pipeline_mode=single),
            pl.BlockSpec(post.shape, whole2, pipeline_mode=single),
            pl.BlockSpec(memory_space=pl.ANY),
            pl.BlockSpec((None, CONV_WIDTH, e), by_w, pipeline_mode=single),
            pl.BlockSpec(conv_b.shape, whole2, pipeline_mode=single),
            pl.BlockSpec((None, GROUPS, hw, hw), by_w4, pipeline_mode=single),
            pl.BlockSpec(ga_b.shape, whole2, pipeline_mode=single),
            pl.BlockSpec((None, GROUPS, hw, hw), by_w4, pipeline_mode=single),
            pl.BlockSpec(gx_b.shape, whole2, pipeline_mode=single),
            pl.BlockSpec(lam.shape, whole2, pipeline_mode=single),
            pl.BlockSpec(memory_space=pl.ANY),
        ],
        out_specs=pl.BlockSpec(memory_space=pl.ANY),
        out_shape=jax.ShapeDtypeStruct((nb, s, d), jnp.float32),
        scratch_shapes=[
            pltpu.VMEM((X_SLOTS, tt, nb, d), jnp.float32),
            pltpu.VMEM((O_SLOTS, tt, nb, d), jnp.float32),
            pltpu.SemaphoreType.DMA((X_SLOTS, nb)),
            pltpu.SemaphoreType.DMA((O_SLOTS, nb)),
            pltpu.VMEM((2 * e // MXU_N, d, MXU_N), jnp.bfloat16),
            pltpu.VMEM((d // MXU_N, e, MXU_N), jnp.bfloat16),
            pltpu.SemaphoreType.DMA((2 * e // MXU_N + d // MXU_N,)),
            pltpu.VMEM((tm, d), jnp.float32),
            pltpu.VMEM((tt + CONV_WIDTH - 1, nb, e), jnp.float32),
            pltpu.VMEM((tt, nb, e), jnp.float32),
            pltpu.VMEM((tm, e), jnp.float32),
            pltpu.VMEM((nb, e), jnp.float32),
        ],
        compiler_params=pltpu.CompilerParams(
            dimension_semantics=("arbitrary",),
            vmem_limit_bytes=VMEM_LIMIT_BYTES),
        name="rglru_layer",
    )(x, mod, pre, post, w_in, conv_w, conv_b, ga_w, ga_b, gx_w, gx_b, lam, w_out)


def kernel(x, c, mod_w, mod_b, pre_norm, post_norm, a_w_in, a_v_norm, a_w_s, a_b_s, a_w_out,
           b_w_in, b_conv_w, b_conv_b, b_ga_w, b_ga_b, b_gx_w, b_gx_b, b_lambda, b_w_out):
    depth = pre_norm.shape[0]
    bf16 = jnp.bfloat16
    mod = _mod_call(c, mod_w, mod_b)
    e = b_w_out.shape[1]
    b_in_scale = jnp.where(jnp.arange(2 * e) < e, 1.0, 0.5).astype(b_w_in.dtype)
    a_w_in, a_w_out = (0.5 * a_w_in).astype(bf16), a_w_out.astype(bf16)
    b_w_in, b_w_out = (b_w_in * b_in_scale).astype(bf16), b_w_out.astype(bf16)
    b_ga_w, b_gx_w = b_ga_w.astype(bf16), b_gx_w.astype(bf16)
    a_bs_lanes = jnp.broadcast_to(a_b_s[..., None], a_b_s.shape + (128,))
    for layer in range(depth):
        j = layer // 2
        if layer % 2 == 0:
            x = _layer_a(x, mod, layer, j, pre_norm, post_norm, a_w_in, a_v_norm, a_w_s,
                         a_bs_lanes, a_w_out, tm=512)
        else:
            x = _layer_b(x, mod, layer, j, pre_norm, post_norm, b_w_in, b_conv_w, b_conv_b,
                         b_ga_w, b_ga_b, b_gx_w, b_gx_b, b_lambda, b_w_out, tt=64)
    return x
```
